```python
import jax, jax.numpy as jnp
from jax import lax
import numpy as np

D_MODEL = 1024
BATCH = 2
SEQ = 8192
DEPTH = 1

CHUNK = 64
MIX_WIDTH = D_MODEL
LRU_WIDTH = MIX_WIDTH // 2
LRU_BLOCKS = 8
LRU_BLOCK = LRU_WIDTH // LRU_BLOCKS
CONV_WIDTH = 4
LRU_C = 8.0
ATT_WIDTH = MIX_WIDTH - LRU_WIDTH
ATT_HEADS = 8
ATT_HEAD_DIM = ATT_WIDTH // ATT_HEADS
LEFT_CHUNKS = 8
BAND_CHUNKS = LEFT_CHUNKS + 1
REL_CLIP = 128
MEM_LEN = 256
MEM_HEADS = 4
MEM_HEAD_DIM = D_MODEL // MEM_HEADS
PEER_HEADS = 8
N_KEYS = 128
N_EXPERTS = N_KEYS * N_KEYS
PEER_TOPK = 16
PEER_KEY_DIM = 128
PEER_QUERY_DIM = 2 * PEER_KEY_DIM
PEER_BLOCK = 128
IN_WIDTH = 2 * LRU_WIDTH + 3 * ATT_WIDTH
EPS = 1e-6
NEG_INF = -1e30

kernel_name = "hybrid_rglru_chunkattn_peer_block"


def _rmsnorm(x, g):
    xf = x.astype(jnp.float32)
    y = xf * lax.rsqrt(jnp.mean(xf * xf, axis=-1, keepdims=True) + EPS)
    return (y * g.astype(jnp.float32)).astype(x.dtype)


def _rg_lru(xr, conv_w, conv_b, gate_a_w, gate_a_b, gate_x_w, gate_x_b, lru_lambda):
    B, S, _ = xr.shape
    xc = lax.conv_general_dilated(
        xr, conv_w[:, None, :], window_strides=(1,), padding=[(CONV_WIDTH - 1, 0)],
        dimension_numbers=('NWC', 'WIO', 'NWC'), feature_group_count=LRU_WIDTH) + conv_b
    xb = xc.reshape(B, S, LRU_BLOCKS, LRU_BLOCK)
    r = jax.nn.sigmoid((jnp.einsum('bshi,hij->bshj', xb, gate_a_w).reshape(B, S, LRU_WIDTH)
                        + gate_a_b).astype(jnp.float32))
    i = jax.nn.sigmoid((jnp.einsum('bshi,hij->bshj', xb, gate_x_w).reshape(B, S, LRU_WIDTH)
                        + gate_x_b).astype(jnp.float32))
    log_a = -LRU_C * r * jax.nn.softplus(-lru_lambda.astype(jnp.float32))
    a = jnp.exp(log_a)
    mult = jnp.sqrt(jnp.maximum(-jnp.expm1(2.0 * log_a), 0.0))
    b = mult * i * xc.astype(jnp.float32)

    def combine(left, right):
        a1, b1 = left
        a2, b2 = right
        return a1 * a2, a2 * b1 + b2

    _, h = lax.associative_scan(combine, (a, b), axis=1)
    return h.astype(xr.dtype)


def _chunk_attention(q, k, v, rel_bias):
    B, S, _ = q.shape
    nc = S // CHUNK
    band = BAND_CHUNKS * CHUNK

    def split(t):
        return t.reshape(B, nc, CHUNK, ATT_HEADS, ATT_HEAD_DIM)

    q, k, v = split(q), split(k), split(v)
    pad = ((0, 0), (LEFT_CHUNKS, 0), (0, 0), (0, 0), (0, 0))
    kp, vp = jnp.pad(k, pad), jnp.pad(v, pad)
    k_band = jnp.concatenate([kp[:, j:j + nc] for j in range(BAND_CHUNKS)], axis=2)
    v_band = jnp.concatenate([vp[:, j:j + nc] for j in range(BAND_CHUNKS)], axis=2)
    s = jnp.einsum('bnqhd,bnkhd->bnhqk', q, k_band).astype(jnp.float32) * (ATT_HEAD_DIM ** -0.5)
    rel = LEFT_CHUNKS * CHUNK + np.arange(CHUNK)[:, None] - np.arange(band)[None, :]
    rel_idx = np.clip(rel, -REL_CLIP, REL_CLIP) + REL_CLIP
    bias = rel_bias[:, rel_idx].astype(jnp.float32)
    key_chunk = (np.arange(nc)[:, None] - LEFT_CHUNKS
                 + (np.arange(band) // CHUNK)[None, :])
    valid = key_chunk >= 0
    s = jnp.where(valid[None, :, None, None, :], s + bias[None, None], NEG_INF)
    p = jax.nn.softmax(s, axis=-1).astype(v.dtype)
    o = jnp.einsum('bnhqk,bnkhd->bnqhd', p, v_band)
    return o.reshape(B, S, ATT_WIDTH)


def _memory_cross_attention(h, mem_n, w_q, w_kv, w_o):
    B, S, D = h.shape
    M = mem_n.shape[1]
    q = (h @ w_q).reshape(B, S, MEM_HEADS, MEM_HEAD_DIM)
    kv = (mem_n @ w_kv).reshape(B, M, 2, MEM_HEADS, MEM_HEAD_DIM)
    k, v = kv[:, :, 0], kv[:, :, 1]
    s = jnp.einsum('bshd,bmhd->bhsm', q, k).astype(jnp.float32) * (MEM_HEAD_DIM ** -0.5)
    p = jax.nn.softmax(s, axis=-1).astype(h.dtype)
    o = jnp.einsum('bhsm,bmhd->bshd', p, v).reshape(B, S, D)
    return o @ w_o


def _peer(h, w_query, sub_keys, expert_u, expert_v):
    B, S, D = h.shape
    q = (h @ w_query).reshape(B, S, PEER_HEADS, 2, PEER_KEY_DIM)
    sc = jnp.einsum('bshpk,hpnk->bshpn', q, sub_keys).astype(jnp.float32)
    v1, i1 = lax.top_k(sc[..., 0, :], PEER_TOPK)
    v2, i2 = lax.top_k(sc[..., 1, :], PEER_TOPK)
    n_cand = PEER_TOPK * PEER_TOPK
    cand = (v1[..., :, None] + v2[..., None, :]).reshape(B, S, PEER_HEADS, n_cand)
    cand_idx = (i1[..., :, None] * N_KEYS + i2[..., None, :]).reshape(B, S, PEER_HEADS, n_cand)
    top_s, pos = lax.top_k(cand, PEER_TOPK)
    idx = jnp.take_along_axis(cand_idx, pos, axis=-1)
    g = jax.nn.softmax(top_s, axis=-1).astype(h.dtype)
    n_sel = PEER_HEADS * PEER_TOPK
    nb = (B * S) // PEER_BLOCK
    hb = h.reshape(nb, PEER_BLOCK, D)
    ib = idx.reshape(nb, PEER_BLOCK, n_sel)
    gb = g.reshape(nb, PEER_BLOCK, n_sel)

    def block(args):
        xt, it, gt = args
        u = expert_u[it]
        act = jax.nn.gelu(jnp.einsum('td,tkd->tk', xt, u))
        return jnp.einsum('tk,tkd->td', gt * act, expert_v[it])

    out = lax.map(block, (hb, ib, gb))
    return out.reshape(B, S, D)


def setup_inputs(seed: int = 0) -> dict:
    key = jax.random.key(seed)
    ks = jax.random.split(key, 32)
    f32 = jnp.float32
    L = DEPTH

    def nrm(k, shape, scale):
        return jax.random.normal(k, shape, f32) * scale

    def gain(k, n):
        return 1.0 + 0.02 * jax.random.normal(k, (L, n), f32)

    a8 = jax.random.uniform(ks[10], (L, LRU_WIDTH), f32, 0.9, 0.999)
    s = a8 ** (1.0 / LRU_C)
    lru_lambda = jnp.log(s) - jnp.log1p(-s)
    return {
        "x": nrm(ks[0], (BATCH, SEQ, D_MODEL), 1.0),
        "mem": nrm(ks[1], (BATCH, MEM_LEN, D_MODEL), 1.0),
        "norm_mix": gain(ks[2], D_MODEL),
        "w_in": nrm(ks[3], (L, D_MODEL, IN_WIDTH), D_MODEL ** -0.5),
        "conv_w": nrm(ks[4], (L, CONV_WIDTH, LRU_WIDTH), CONV_WIDTH ** -0.5),
        "conv_b": nrm(ks[5], (L, LRU_WIDTH), 0.01),
        "gate_a_w": nrm(ks[6], (L, LRU_BLOCKS, LRU_BLOCK, LRU_BLOCK), LRU_BLOCK ** -0.5),
        "gate_a_b": nrm(ks[7], (L, LRU_WIDTH), 0.01),
        "gate_x_w": nrm(ks[8], (L, LRU_BLOCKS, LRU_BLOCK, LRU_BLOCK), LRU_BLOCK ** -0.5),
        "gate_x_b": nrm(ks[9], (L, LRU_WIDTH), 0.01),
        "lru_lambda": lru_lambda,
        "rel_bias": nrm(ks[11], (L, ATT_HEADS, 2 * REL_CLIP + 1), 0.5),
        "norm_grp_a": gain(ks[12], LRU_WIDTH),
        "norm_grp_b": gain(ks[13], ATT_WIDTH),
        "w_out": nrm(ks[14], (L, MIX_WIDTH, D_MODEL), MIX_WIDTH ** -0.5),
        "norm_cross": gain(ks[15], D_MODEL),
        "norm_mem": gain(ks[16], D_MODEL),
        "w_q_mem": nrm(ks[17], (L, D_MODEL, D_MODEL), D_MODEL ** -0.5),
        "w_kv_mem": nrm(ks[18], (L, D_MODEL, 2 * D_MODEL), D_MODEL ** -0.5),
        "w_o_mem": nrm(ks[19], (L, D_MODEL, D_MODEL), D_MODEL ** -0.5),
        "norm_ffn": gain(ks[20], D_MODEL),
        "w_query": nrm(ks[21], (L, D_MODEL, PEER_HEADS * PEER_QUERY_DIM), D_MODEL ** -0.5),
        "sub_keys": nrm(ks[22], (L, PEER_HEADS, 2, N_KEYS, PEER_KEY_DIM), PEER_KEY_DIM ** -0.5),
        "expert_u": nrm(ks[23], (L, N_EXPERTS, D_MODEL), D_MODEL ** -0.5),
        "expert_v": nrm(ks[24], (L, N_EXPERTS, D_MODEL), PEER_HEADS ** -0.5),
        "norm_final": 1.0 + 0.02 * jax.random.normal(ks[25], (D_MODEL,), f32),
    }


def reference(x, mem, norm_mix, w_in, conv_w, conv_b, gate_a_w, gate_a_b, gate_x_w, gate_x_b,
              lru_lambda, rel_bias, norm_grp_a, norm_grp_b, w_out, norm_cross, norm_mem,
              w_q_mem, w_kv_mem, w_o_mem, norm_ffn, w_query, sub_keys, expert_u, expert_v,
              norm_final):
    splits = [LRU_WIDTH, 2 * LRU_WIDTH, 2 * LRU_WIDTH + ATT_WIDTH, 2 * LRU_WIDTH + 2 * ATT_WIDTH]
    for l in range(DEPTH):
        hn = _rmsnorm(x, norm_mix[l])
        z = hn @ w_in[l]
        x_lru, gate, q, k, v = jnp.split(z, splits, axis=-1)
        y_a = _rg_lru(x_lru, conv_w[l], conv_b[l], gate_a_w[l], gate_a_b[l],
                      gate_x_w[l], gate_x_b[l], lru_lambda[l]) * jax.nn.gelu(gate)
        y_b = _chunk_attention(q, k, v, rel_bias[l])
        y = jnp.concatenate([_rmsnorm(y_a, norm_grp_a[l]), _rmsnorm(y_b, norm_grp_b[l])], axis=-1)
        x = x + y @ w_out[l]
        x = x + _memory_cross_attention(_rmsnorm(x, norm_cross[l]), _rmsnorm(mem, norm_mem[l]),
                                        w_q_mem[l], w_kv_mem[l], w_o_mem[l])
        x = x + _peer(_rmsnorm(x, norm_ffn[l]), w_query[l], sub_keys[l], expert_u[l], expert_v[l])
    return _rmsnorm(x, norm_final)
```

```python
import functools

import jax
import jax.numpy as jnp
import numpy as np
from jax import lax
from jax.experimental import pallas as pl
from jax.experimental.pallas import tpu as pltpu

F32 = jnp.float32
BF16 = jnp.bfloat16

EPS = 1e-6
NEG_INF = -1e30

CHUNK = 64
LEFT_CHUNKS = 8
REL_CLIP = 128
CONV_WIDTH = 4
LRU_C = 8.0
ATT_HEADS = 8
MEM_HEADS = 4
PEER_HEADS = 8
N_KEYS = 128
PEER_TOPK = 16

SUBLANES = 8
LANES = 128
VMEM_LIMIT_BYTES = 56 * 1024 * 1024

TOKEN_TILE = 512
LRU_TILE = 256
ATT_TILE = 4 * CHUNK
ATT_WINDOW = ATT_TILE + LEFT_CHUNKS * CHUNK
ROUTE_TILE = 256
DENSE_TOKEN_TILE = 512
DENSE_EXPERT_TILE = 512


def _params(*semantics):
    return pltpu.CompilerParams(dimension_semantics=semantics,
                                vmem_limit_bytes=VMEM_LIMIT_BYTES)


def _rms(xf, g):
    return xf * lax.rsqrt(jnp.mean(xf * xf, axis=-1, keepdims=True) + EPS) * g


def _const_spec(shape):
    nd = len(shape)
    return pl.BlockSpec(shape, lambda *_: (0,) * nd)


def _in_proj_kernel(x_ref, g_ref, w_ref, zl_ref, q_ref, k_ref, v_ref):
    hn = _rms(x_ref[...], g_ref[...]).astype(BF16)
    z = jnp.dot(hn, w_ref[...], preferred_element_type=F32)
    lw = zl_ref.shape[-1]
    aw = q_ref.shape[-1]
    zl_ref[...] = z[:, :lw]
    q_ref[...] = z[:, lw:lw + aw].astype(BF16)
    k_ref[...] = z[:, lw + aw:lw + 2 * aw].astype(BF16)
    v_ref[...] = z[:, lw + 2 * aw:].astype(BF16)


def _in_proj(x2d, g, w_in, lru_w2, att_w):
    T, D = x2d.shape
    tm = TOKEN_TILE
    row = lambda w: pl.BlockSpec((tm, w), lambda i: (i, 0))
    return pl.pallas_call(
        _in_proj_kernel,
        grid=(T // tm,),
        in_specs=[row(D), _const_spec((1, D)), _const_spec(w_in.shape)],
        out_specs=[row(lru_w2), row(att_w), row(att_w), row(att_w)],
        out_shape=[jax.ShapeDtypeStruct((T, lru_w2), F32)]
        + [jax.ShapeDtypeStruct((T, att_w), BF16)] * 3,
        compiler_params=_params("parallel"),
    )(x2d, g, w_in)


def _rg_lru_kernel(z_ref, cw_ref, cb_ref, wg_ref, bg_ref, lam_ref, gn_ref, o_ref,
                   xbuf, hcar):
    ts, w = o_ref.shape
    hist = SUBLANES

    @pl.when(pl.program_id(1) == 0)
    def _():
        xbuf[0:hist, :] = jnp.zeros((hist, w), F32)
        hcar[...] = jnp.zeros_like(hcar)

    xl = z_ref[:, :w]
    gate = z_ref[:, w:]
    xbuf[hist:hist + ts, :] = xl
    xc = cb_ref[...] + cw_ref[CONV_WIDTH - 1:CONV_WIDTH, :] * xl
    for j in range(CONV_WIDTH - 1):
        off = hist - (CONV_WIDTH - 1) + j
        xc = xc + cw_ref[j:j + 1, :] * xbuf[off:off + ts, :]
    xbuf[0:hist, :] = xbuf[ts:ts + hist, :]

    gates = jnp.dot(xc.astype(BF16), wg_ref[...], preferred_element_type=F32) + bg_ref[...]
    r = jax.nn.sigmoid(gates[:, :w])
    ig = jax.nn.sigmoid(gates[:, w:])
    lam = lam_ref[...]
    sp = jnp.log1p(jnp.exp(-jnp.abs(lam))) + jnp.maximum(-lam, 0.0)
    log_a = (-LRU_C) * r * sp
    a = jnp.exp(log_a)
    mult = jnp.sqrt(jnp.maximum(1.0 - a * a, 0.0))
    b = mult * ig * xc

    rows = lax.broadcasted_iota(jnp.int32, (ts, w), 0)
    d = 1
    while d < ts:
        keep = rows >= d
        a_prev = jnp.where(keep, pltpu.roll(a, d, 0), 1.0)
        b_prev = jnp.where(keep, pltpu.roll(b, d, 0), 0.0)
        b = a * b_prev + b
        a = a * a_prev
        d *= 2
    h = b + a * hcar[0:1, :]
    hcar[...] = jnp.broadcast_to(h[ts - 1:ts, :], hcar.shape)

    y = h * jax.nn.gelu(gate)
    o_ref[...] = _rms(y, gn_ref[...]).astype(o_ref.dtype)


def _rg_lru(zl, batch, conv_w, conv_b, w_gates, b_gates, lam, g_norm):
    T, w2 = zl.shape
    w = w2 // 2
    ts = LRU_TILE
    nt = T // batch // ts
    return pl.pallas_call(
        _rg_lru_kernel,
        grid=(batch, nt),
        in_specs=[pl.BlockSpec((ts, w2), lambda b, s: (b * nt + s, 0)),
                  _const_spec(conv_w.shape), _const_spec((1, w)),
                  _const_spec(w_gates.shape), _const_spec((1, w2)),
                  _const_spec((1, w)), _const_spec((1, w))],
        out_specs=pl.BlockSpec((ts, w), lambda b, s: (b * nt + s, 0)),
        out_shape=jax.ShapeDtypeStruct((T, w), BF16),
        scratch_shapes=[pltpu.VMEM((ts + SUBLANES, w), F32), pltpu.VMEM((SUBLANES, w), F32)],
        compiler_params=_params("parallel", "arbitrary"),
    )(zl, conv_w, conv_b, w_gates, b_gates, lam, g_norm)


def _chunk_attn_kernel(q_ref, k0_ref, k1_ref, k2_ref, v0_ref, v1_ref, v2_ref, bias_ref,
                       gn_ref, o_ref):
    tq, aw = q_ref.shape
    m = pl.program_id(1)
    dh = aw // ATT_HEADS
    scale = dh ** -0.5
    kw = 3 * tq
    col_blk = lax.broadcasted_iota(jnp.int32, (tq, kw), 1) // tq
    valid = (col_blk + m) >= 2
    lane = lax.broadcasted_iota(jnp.int32, (1, LANES), 1)
    outs = []
    for pair in range(aw // LANES):
        sl = slice(pair * LANES, (pair + 1) * LANES)
        qp = q_ref[:, sl]
        kp = jnp.concatenate([k0_ref[:, sl], k1_ref[:, sl], k2_ref[:, sl]], axis=0)
        vp = jnp.concatenate([v0_ref[:, sl], v1_ref[:, sl], v2_ref[:, sl]], axis=0)
        acc = jnp.zeros((tq, LANES), F32)
        for e in range(LANES // dh):
            head = pair * (LANES // dh) + e
            sel = (lane >= e * dh) & (lane < (e + 1) * dh)
            qm = jnp.where(sel, qp, jnp.zeros_like(qp))
            vm = jnp.where(sel, vp, jnp.zeros_like(vp))
            s = lax.dot_general(qm, kp, (((1,), (1,)), ((), ())),
                                preferred_element_type=F32)
            s = jnp.where(valid, s * scale + bias_ref[head], NEG_INF)
            mx = jnp.max(s, axis=-1, keepdims=True)
            p = jnp.exp(s - mx)
            l = jnp.sum(p, axis=-1, keepdims=True)
            o = jnp.dot(p.astype(BF16), vm, preferred_element_type=F32)
            acc = acc + o * (1.0 / l)
        outs.append(acc)
    y = jnp.concatenate(outs, axis=-1)
    o_ref[...] = _rms(y, gn_ref[...]).astype(o_ref.dtype)


def _attention_bias(rel_bias):
    tq = ATT_TILE
    qpos = np.arange(tq)[:, None]
    kpos = np.arange(3 * tq)[None, :] - 2 * tq
    rel = qpos - kpos
    idx = np.clip(rel, -REL_CLIP, REL_CLIP) + REL_CLIP
    qc = qpos // CHUNK
    kc = np.floor_divide(kpos, CHUNK)
    band = (kc <= qc) & (kc >= qc - LEFT_CHUNKS)
    gathered = rel_bias[:, idx].astype(F32)
    return jnp.where(band[None], gathered, NEG_INF)


def _chunk_attn(q, k, v, batch, bias, g_norm):
    T, aw = q.shape
    tq = ATT_TILE
    nt = T // batch // tq
    qspec = pl.BlockSpec((tq, aw), lambda b, m: (b * nt + m, 0))

    def kspec(j):
        return pl.BlockSpec((tq, aw), lambda b, m: (b * nt + jnp.maximum(m - 2 + j, 0), 0))

    return pl.pallas_call(
        _chunk_attn_kernel,
        grid=(batch, nt),
        in_specs=[qspec, kspec(0), kspec(1), kspec(2), kspec(0), kspec(1), kspec(2),
                  _const_spec(bias.shape), _const_spec((1, aw))],
        out_specs=qspec,
        out_shape=jax.ShapeDtypeStruct((T, aw), BF16),
        compiler_params=_params("parallel", "parallel"),
    )(q, k, k, k, v, v, v, bias, g_norm)


def _mem_kv_kernel(m_ref, g_ref, w_ref, k_ref, v_ref):
    mn = _rms(m_ref[...], g_ref[...]).astype(BF16)
    kv = jnp.dot(mn, w_ref[...], preferred_element_type=F32)
    d = k_ref.shape[-1]
    k_ref[...] = kv[:, :d].astype(BF16)
    v_ref[...] = kv[:, d:].astype(BF16)


def _mem_kv(mem2d, batch, g, w_kv):
    R, D = mem2d.shape
    ml = R // batch
    row = pl.BlockSpec((ml, D), lambda b: (b, 0))
    return pl.pallas_call(
        _mem_kv_kernel,
        grid=(batch,),
        in_specs=[row, _const_spec((1, D)), _const_spec(w_kv.shape)],
        out_specs=[row, row],
        out_shape=[jax.ShapeDtypeStruct((R, D), BF16)] * 2,
        compiler_params=_params("parallel"),
    )(mem2d, g, w_kv)


def _mix_cross_kernel(x_ref, ya_ref, yb_ref, woa_ref, wob_ref, gc_ref, wq_ref, km_ref,
                      vm_ref, wo_ref, gf_ref, wqp_ref, x2_ref, hn_ref, qp_ref):
    x1 = (x_ref[...]
          + jnp.dot(ya_ref[...], woa_ref[...], preferred_element_type=F32)
          + jnp.dot(yb_ref[...], wob_ref[...], preferred_element_type=F32))
    hn2 = _rms(x1, gc_ref[...]).astype(BF16)
    q = jnp.dot(hn2, wq_ref[...], preferred_element_type=F32).astype(BF16)
    d = q.shape[-1]
    dh = d // MEM_HEADS
    scale = dh ** -0.5
    heads = []
    for h in range(MEM_HEADS):
        sl = slice(h * dh, (h + 1) * dh)
        s = lax.dot_general(q[:, sl], km_ref[:, sl], (((1,), (1,)), ((), ())),
                            preferred_element_type=F32) * scale
        mx = jnp.max(s, axis=-1, keepdims=True)
        p = jnp.exp(s - mx)
        l = jnp.sum(p, axis=-1, keepdims=True)
        o = jnp.dot(p.astype(BF16), vm_ref[:, sl], preferred_element_type=F32) * (1.0 / l)
        heads.append(o.astype(BF16))
    o_all = jnp.concatenate(heads, axis=-1)
    x2 = x1 + jnp.dot(o_all, wo_ref[...], preferred_element_type=F32)
    x2_ref[...] = x2
    hn3 = _rms(x2, gf_ref[...]).astype(BF16)
    hn_ref[...] = hn3
    qp_ref[...] = jnp.dot(hn3, wqp_ref[...], preferred_element_type=F32).astype(BF16)


def _mix_cross(x2d, ya, yb, wo_a, wo_b, g_cross, w_q, kmem, vmem, w_o, g_ffn, w_qp, batch):
    T, D = x2d.shape
    tm = TOKEN_TILE
    per_batch = T // batch // tm
    ml = kmem.shape[0] // batch
    row = lambda w: pl.BlockSpec((tm, w), lambda i: (i, 0))
    mem_spec = pl.BlockSpec((ml, D), lambda i: (i // per_batch, 0))
    qw = w_qp.shape[1]
    return pl.pallas_call(
        _mix_cross_kernel,
        grid=(T // tm,),
        in_specs=[row(D), row(ya.shape[1]), row(yb.shape[1]),
                  _const_spec(wo_a.shape), _const_spec(wo_b.shape), _const_spec((1, D)),
                  _const_spec(w_q.shape), mem_spec, mem_spec, _const_spec(w_o.shape),
                  _const_spec((1, D)), _const_spec(w_qp.shape)],
        out_specs=[row(D), row(D), row(qw)],
        out_shape=[jax.ShapeDtypeStruct((T, D), F32), jax.ShapeDtypeStruct((T, D), BF16),
                   jax.ShapeDtypeStruct((T, qw), BF16)],
        compiler_params=_params("parallel"),
    )(x2d, ya, yb, wo_a, wo_b, g_cross, w_q, kmem, vmem, w_o, g_ffn, w_qp)


def _vmax(a, b):
    if a is None:
        return b
    if b is None:
        return a
    return jnp.maximum(a, b)


def _vmin(a, b):
    if a is None or b is None:
        return None
    return jnp.minimum(a, b)


def _bitonic_merge_desc(vals):
    n = len(vals)
    vals = list(vals)
    j = n // 2
    while j >= 1:
        for i in range(n):
            l = i ^ j
            if l > i:
                hi, lo = _vmax(vals[i], vals[l]), _vmin(vals[i], vals[l])
                vals[i], vals[l] = hi, lo
        j //= 2
    return vals


def _bitonic_sort_desc(vals):
    n = len(vals)
    vals = list(vals)
    k = 2
    while k <= n:
        j = k // 2
        while j >= 1:
            for i in range(n):
                l = i ^ j
                if l > i:
                    hi, lo = _vmax(vals[i], vals[l]), _vmin(vals[i], vals[l])
                    if (i & k) == 0:
                        vals[i], vals[l] = hi, lo
                    else:
                        vals[i], vals[l] = lo, hi
            j //= 2
        k *= 2
    return vals


def _top16_desc(sc):
    n = sc.shape[-1]
    k = PEER_TOPK
    sc3 = sc.reshape(N_KEYS // SUBLANES, SUBLANES, n)
    vals = _bitonic_sort_desc([sc3[i] for i in range(k)])
    shift = SUBLANES // 2
    while shift >= 1:
        other = [pltpu.roll(v, shift, 0) for v in vals]
        vals = _bitonic_merge_desc([jnp.maximum(vals[i], other[k - 1 - i]) for i in range(k)])
        shift //= 2
    return vals


_CAND = [(a, b) for a in range(PEER_TOPK) for b in range(PEER_TOPK)
         if (a + 1) * (b + 1) <= PEER_TOPK]


def _peer_route_kernel(qp_ref, keys_ref, nsel_ref, c1_ref, rank2_ref, e2_ref):
    tn = qp_ref.shape[0]
    k = PEER_TOPK
    kd = keys_ref.shape[-1]
    sub = lax.broadcasted_iota(jnp.int32, (SUBLANES, tn), 0)

    scores = []
    tops = []
    for hp in range(2 * PEER_HEADS):
        qh = qp_ref[:, hp * kd:(hp + 1) * kd]
        sc = lax.dot_general(keys_ref[hp], qh, (((1,), (1,)), ((), ())),
                             preferred_element_type=F32)
        scores.append(sc)
        tops.append(_top16_desc(sc))

    def pack(p, a):
        out = tops[p][a]
        for h in range(1, PEER_HEADS):
            out = jnp.where(sub == h, tops[2 * h + p][a], out)
        return out

    V1 = [pack(0, a) for a in range(k)]
    V2 = [pack(1, b) for b in range(k)]
    cand = {ab: V1[ab[0]] + V2[ab[1]] for ab in _CAND}
    padded = [cand[ab] for ab in _CAND] + [None] * (64 - len(_CAND))
    tau = _bitonic_sort_desc(padded)[k - 1]
    E1 = [jnp.exp(V1[a] - V1[0]) for a in range(k)]
    E2 = [jnp.exp(V2[b] - V2[0]) for b in range(k)]
    zsum = jnp.zeros((SUBLANES, tn), F32)
    ncnt = [jnp.zeros((SUBLANES, tn), F32) for _ in range(k + 1)]
    for (a, b) in _CAND:
        hit = cand[(a, b)] >= tau
        zsum = zsum + jnp.where(hit, E1[a] * E2[b], 0.0)
        ncnt[a] = ncnt[a] + jnp.where(hit, 1.0, 0.0)
    zinv = 1.0 / zsum
    dn = [ncnt[a] - ncnt[a + 1] for a in range(k)]

    for h in range(PEER_HEADS):
        s1 = scores[2 * h]
        s2 = scores[2 * h + 1]
        row = lambda arr: arr[h:h + 1, :]
        nsel = jnp.zeros_like(s1)
        for a in range(k):
            nsel = nsel + jnp.where(s1 >= row(V1[a]), row(dn[a]), 0.0)
        rank2 = jnp.zeros_like(s2)
        for b in range(k):
            rank2 = rank2 + jnp.where(s2 < row(V2[b]), 1.0, 0.0)
        nsel_ref[:, h, :] = nsel
        c1_ref[:, h, :] = jnp.exp(s1 - row(V1[0])) * row(zinv)
        rank2_ref[h] = rank2.astype(rank2_ref.dtype)
        e2_ref[h] = jnp.exp(s2 - row(V2[0])).astype(e2_ref.dtype)


def _peer_route(qp, keys):
    T = qp.shape[0]
    tn = ROUTE_TILE
    i1_major = pl.BlockSpec((N_KEYS, PEER_HEADS, tn), lambda i: (0, 0, i))
    head_major = pl.BlockSpec((PEER_HEADS, N_KEYS, tn), lambda i: (0, 0, i))
    return pl.pallas_call(
        _peer_route_kernel,
        grid=(T // tn,),
        in_specs=[pl.BlockSpec((tn, qp.shape[1]), lambda i: (i, 0)), _const_spec(keys.shape)],
        out_specs=[i1_major, i1_major, head_major, head_major],
        out_shape=[jax.ShapeDtypeStruct((N_KEYS, PEER_HEADS, T), F32)] * 2
        + [jax.ShapeDtypeStruct((PEER_HEADS, N_KEYS, T), BF16)] * 2,
        compiler_params=_params("parallel"),
    )(qp, keys)


def _peer_dense_kernel(hn_ref, u_ref, vt_ref, nsel_ref, c1_ref, rank2_ref, e2_ref, x2_ref,
                       gn_ref, o_ref, acc_ref, *, final_norm):
    j = pl.program_id(1)
    te = u_ref.shape[0]
    tm = hn_ref.shape[0]

    @pl.when(j == 0)
    def _():
        acc_ref[...] = jnp.zeros_like(acc_ref)

    act = lax.dot_general(u_ref[...], hn_ref[...], (((1,), (1,)), ((), ())),
                          preferred_element_type=F32)
    act = jax.nn.gelu(act)
    groups = te // N_KEYS
    hs = []
    for g in range(groups):
        i1 = j * groups + g
        nsel = nsel_ref[i1].astype(BF16)
        c1 = c1_ref[i1].astype(BF16)
        gate = jnp.zeros((N_KEYS, tm), BF16)
        for h in range(PEER_HEADS):
            n_b = jnp.broadcast_to(nsel[h:h + 1, :], (N_KEYS, tm))
            c_b = jnp.broadcast_to(c1[h:h + 1, :], (N_KEYS, tm))
            gate = gate + jnp.where(rank2_ref[h] < n_b, e2_ref[h], jnp.zeros_like(c_b)) * c_b
        hs.append(gate * act[g * N_KEYS:(g + 1) * N_KEYS, :].astype(BF16))
    hmat = jnp.concatenate(hs, axis=0)
    acc_ref[...] += jnp.dot(vt_ref[...], hmat, preferred_element_type=F32)

    @pl.when(j == pl.num_programs(1) - 1)
    def _():
        y = x2_ref[...] + acc_ref[...].T
        o_ref[...] = _rms(y, gn_ref[...]) if final_norm else y


def _peer_dense(hn3, u, vt, nsel, c1, rank2, e2, x2, g_final, final_norm):
    T, D = hn3.shape
    E = u.shape[0]
    tm, te = DENSE_TOKEN_TILE, DENSE_EXPERT_TILE
    i1_major = pl.BlockSpec((N_KEYS, PEER_HEADS, tm), lambda i, j: (0, 0, i))
    head_major = pl.BlockSpec((PEER_HEADS, N_KEYS, tm), lambda i, j: (0, 0, i))
    row = pl.BlockSpec((tm, D), lambda i, j: (i, 0))
    return pl.pallas_call(
        functools.partial(_peer_dense_kernel, final_norm=final_norm),
        grid=(T // tm, E // te),
        in_specs=[row, pl.BlockSpec((te, D), lambda i, j: (j, 0)),
                  pl.BlockSpec((D, te), lambda i, j: (0, j)),
                  i1_major, i1_major, head_major, head_major, row,
                  pl.BlockSpec((1, D), lambda i, j: (0, 0))],
        out_specs=row,
        out_shape=jax.ShapeDtypeStruct((T, D), F32),
        scratch_shapes=[pltpu.VMEM((D, tm), F32)],
        compiler_params=_params("parallel", "arbitrary"),
    )(hn3, u, vt, nsel, c1, rank2, e2, x2, g_final)


def _block_diag(w):
    nb, bi, bo = w.shape
    eye = jnp.eye(nb, dtype=w.dtype)
    return (eye[:, None, :, None] * w[:, :, None, :]).reshape(nb * bi, nb * bo)


def kernel(x, mem, norm_mix, w_in, conv_w, conv_b, gate_a_w, gate_a_b, gate_x_w, gate_x_b, lru_lambda, rel_bias, norm_grp_a, norm_grp_b, w_out, norm_cross, norm_mem, w_q_mem, w_kv_mem, w_o_mem, norm_ffn, w_query, sub_keys, expert_u, expert_v, norm_final):
    B, S, D = x.shape
    depth = w_in.shape[0]
    lru_w = conv_w.shape[-1]
    att_w = (w_in.shape[-1] - 2 * lru_w) // 3
    row = lambda v: v.reshape(1, -1)

    cur = x.reshape(B * S, D)
    mem2d = mem.reshape(B * mem.shape[1], D)
    for l in range(depth):
        zl, q, k, v = _in_proj(cur, row(norm_mix[l]), w_in[l].astype(BF16), 2 * lru_w, att_w)
        w_gates = jnp.concatenate([_block_diag(gate_a_w[l]), _block_diag(gate_x_w[l])],
                                  axis=1).astype(BF16)
        b_gates = jnp.concatenate([gate_a_b[l], gate_x_b[l]]).reshape(1, -1)
        ya = _rg_lru(zl, B, conv_w[l], row(conv_b[l]), w_gates, b_gates,
                     row(lru_lambda[l]), row(norm_grp_a[l]))
        yb = _chunk_attn(q, k, v, B, _attention_bias(rel_bias[l]), row(norm_grp_b[l]))
        kmem, vmem = _mem_kv(mem2d, B, row(norm_mem[l]), w_kv_mem[l].astype(BF16))
        wo = w_out[l].astype(BF16)
        x2, hn3, qp = _mix_cross(cur, ya, yb, wo[:lru_w], wo[lru_w:], row(norm_cross[l]),
                                 w_q_mem[l].astype(BF16), kmem, vmem,
                                 w_o_mem[l].astype(BF16), row(norm_ffn[l]),
                                 w_query[l].astype(BF16), B)
        keys = sub_keys[l].reshape(2 * PEER_HEADS, N_KEYS, -1).astype(BF16)
        nsel, c1, rank2, e2 = _peer_route(qp, keys)
        cur = _peer_dense(hn3, expert_u[l].astype(BF16), expert_v[l].T.astype(BF16),
                          nsel, c1, rank2, e2, x2, row(norm_final), l == depth - 1)
    return cur.reshape(B, S, D)
```

```python
import functools

import jax
import jax.numpy as jnp
import numpy as np
from jax import lax
from jax.experimental import pallas as pl
from jax.experimental.pallas import tpu as pltpu

F32 = jnp.float32
BF16 = jnp.bfloat16

EPS = 1e-6
NEG_INF = -1e30

CHUNK = 64
LEFT_CHUNKS = 8
REL_CLIP = 128
CONV_WIDTH = 4
LRU_C = 8.0
ATT_HEADS = 8
MEM_HEADS = 4
PEER_HEADS = 8
N_KEYS = 128
PEER_TOPK = 16

SUBLANES = 8
LANES = 128
VMEM_LIMIT_BYTES = 56 * 1024 * 1024

TOKEN_TILE = 512
LRU_TILE = 256
ATT_TILE = 4 * CHUNK
ATT_WINDOW = ATT_TILE + LEFT_CHUNKS * CHUNK
ROUTE_TILE = 256
DENSE_TOKEN_TILE = 512
DENSE_EXPERT_TILE = 2048
DENSE_CHUNK = 512


def _params(*semantics, flags=None):
    return pltpu.CompilerParams(dimension_semantics=semantics,
                                vmem_limit_bytes=VMEM_LIMIT_BYTES, flags=flags)


def _rms(xf, g):
    return xf * lax.rsqrt(jnp.mean(xf * xf, axis=-1, keepdims=True) + EPS) * g


def _const_spec(shape):
    nd = len(shape)
    return pl.BlockSpec(shape, lambda *_: (0,) * nd)


def _in_proj_kernel(x_ref, g_ref, w_ref, zl_ref, q_ref, k_ref, v_ref):
    hn = _rms(x_ref[...], g_ref[...]).astype(BF16)
    z = jnp.dot(hn, w_ref[...], preferred_element_type=F32)
    lw = zl_ref.shape[-1]
    aw = q_ref.shape[-1]
    zl_ref[...] = z[:, :lw]
    q_ref[...] = z[:, lw:lw + aw].astype(BF16)
    k_ref[...] = z[:, lw + aw:lw + 2 * aw].astype(BF16)
    v_ref[...] = z[:, lw + 2 * aw:].astype(BF16)


def _in_proj(x2d, g, w_in, lru_w2, att_w):
    T, D = x2d.shape
    tm = TOKEN_TILE
    row = lambda w: pl.BlockSpec((tm, w), lambda i: (i, 0))
    return pl.pallas_call(
        _in_proj_kernel,
        grid=(T // tm,),
        in_specs=[row(D), _const_spec((1, D)), _const_spec(w_in.shape)],
        out_specs=[row(lru_w2), row(att_w), row(att_w), row(att_w)],
        out_shape=[jax.ShapeDtypeStruct((T, lru_w2), F32)]
        + [jax.ShapeDtypeStruct((T, att_w), BF16)] * 3,
        compiler_params=_params("parallel"),
    )(x2d, g, w_in)


def _rg_lru_kernel(z_ref, cw_ref, cb_ref, wg_ref, bg_ref, lam_ref, gn_ref, o_ref,
                   xbuf, hcar):
    ts, w = o_ref.shape
    hist = SUBLANES

    @pl.when(pl.program_id(1) == 0)
    def _():
        xbuf[0:hist, :] = jnp.zeros((hist, w), F32)
        hcar[...] = jnp.zeros_like(hcar)

    xl = z_ref[:, :w]
    gate = z_ref[:, w:]
    xbuf[hist:hist + ts, :] = xl
    xc = cb_ref[...] + cw_ref[CONV_WIDTH - 1:CONV_WIDTH, :] * xl
    for j in range(CONV_WIDTH - 1):
        off = hist - (CONV_WIDTH - 1) + j
        xc = xc + cw_ref[j:j + 1, :] * xbuf[off:off + ts, :]
    xbuf[0:hist, :] = xbuf[ts:ts + hist, :]

    gates = jnp.dot(xc.astype(BF16), wg_ref[...], preferred_element_type=F32) + bg_ref[...]
    r = jax.nn.sigmoid(gates[:, :w])
    ig = jax.nn.sigmoid(gates[:, w:])
    lam = lam_ref[...]
    sp = jnp.log1p(jnp.exp(-jnp.abs(lam))) + jnp.maximum(-lam, 0.0)
    log_a = (-LRU_C) * r * sp
    a = jnp.exp(log_a)
    mult = jnp.sqrt(jnp.maximum(1.0 - a * a, 0.0))
    b = mult * ig * xc

    rows = lax.broadcasted_iota(jnp.int32, (ts, w), 0)
    d = 1
    while d < ts:
        keep = rows >= d
        a_prev = jnp.where(keep, pltpu.roll(a, d, 0), 1.0)
        b_prev = jnp.where(keep, pltpu.roll(b, d, 0), 0.0)
        b = a * b_prev + b
        a = a * a_prev
        d *= 2
    h = b + a * hcar[0:1, :]
    hcar[...] = jnp.broadcast_to(h[ts - 1:ts, :], hcar.shape)

    y = h * jax.nn.gelu(gate)
    o_ref[...] = _rms(y, gn_ref[...]).astype(o_ref.dtype)


def _rg_lru(zl, batch, conv_w, conv_b, w_gates, b_gates, lam, g_norm):
    T, w2 = zl.shape
    w = w2 // 2
    ts = LRU_TILE
    nt = T // batch // ts
    return pl.pallas_call(
        _rg_lru_kernel,
        grid=(batch, nt),
        in_specs=[pl.BlockSpec((ts, w2), lambda b, s: (b * nt + s, 0)),
                  _const_spec(conv_w.shape), _const_spec((1, w)),
                  _const_spec(w_gates.shape), _const_spec((1, w2)),
                  _const_spec((1, w)), _const_spec((1, w))],
        out_specs=pl.BlockSpec((ts, w), lambda b, s: (b * nt + s, 0)),
        out_shape=jax.ShapeDtypeStruct((T, w), BF16),
        scratch_shapes=[pltpu.VMEM((ts + SUBLANES, w), F32), pltpu.VMEM((SUBLANES, w), F32)],
        compiler_params=_params("parallel", "arbitrary"),
    )(zl, conv_w, conv_b, w_gates, b_gates, lam, g_norm)


def _chunk_attn_kernel(q_ref, k0_ref, k1_ref, k2_ref, v0_ref, v1_ref, v2_ref, bias_ref,
                       gn_ref, o_ref):
    tq, aw = q_ref.shape
    m = pl.program_id(1)
    dh = aw // ATT_HEADS
    scale = dh ** -0.5
    kw = 3 * tq
    col_blk = lax.broadcasted_iota(jnp.int32, (tq, kw), 1) // tq
    valid = (col_blk + m) >= 2
    lane = lax.broadcasted_iota(jnp.int32, (1, LANES), 1)
    outs = []
    for pair in range(aw // LANES):
        sl = slice(pair * LANES, (pair + 1) * LANES)
        qp = q_ref[:, sl]
        kp = jnp.concatenate([k0_ref[:, sl], k1_ref[:, sl], k2_ref[:, sl]], axis=0)
        vp = jnp.concatenate([v0_ref[:, sl], v1_ref[:, sl], v2_ref[:, sl]], axis=0)
        acc = jnp.zeros((tq, LANES), F32)
        for e in range(LANES // dh):
            head = pair * (LANES // dh) + e
            sel = (lane >= e * dh) & (lane < (e + 1) * dh)
            qm = jnp.where(sel, qp, jnp.zeros_like(qp))
            vm = jnp.where(sel, vp, jnp.zeros_like(vp))
            s = lax.dot_general(qm, kp, (((1,), (1,)), ((), ())),
                                preferred_element_type=F32)
            s = jnp.where(valid, s * scale + bias_ref[head], NEG_INF)
            mx = jnp.max(s, axis=-1, keepdims=True)
            p = jnp.exp(s - mx)
            l = jnp.sum(p, axis=-1, keepdims=True)
            o = jnp.dot(p.astype(BF16), vm, preferred_element_type=F32)
            acc = acc + o * (1.0 / l)
        outs.append(acc)
    y = jnp.concatenate(outs, axis=-1)
    o_ref[...] = _rms(y, gn_ref[...]).astype(o_ref.dtype)


def _attention_bias(rel_bias):
    tq = ATT_TILE
    kw = 3 * tq
    qpos = np.arange(tq)[:, None]
    kpos = np.arange(kw)[None, :] - 2 * tq
    qc = qpos // CHUNK
    kc = np.floor_divide(kpos, CHUNK)
    band = (kc <= qc) & (kc >= qc - LEFT_CHUNKS)
    period = 4 * tq
    j = np.arange(period)
    col_minus_row = np.where(j < kw, j, j - period)
    idx = np.clip(2 * tq - col_minus_row, -REL_CLIP, REL_CLIP) + REL_CLIP
    profile = rel_bias[:, idx].astype(F32)
    heads = profile.shape[0]
    sheared = jnp.tile(profile, (1, tq))[:, :tq * (period - 1)].reshape(heads, tq, period - 1)
    return jnp.where(band[None], sheared[:, :, :kw], NEG_INF)


def _chunk_attn(q, k, v, batch, bias, g_norm):
    T, aw = q.shape
    tq = ATT_TILE
    nt = T // batch // tq
    qspec = pl.BlockSpec((tq, aw), lambda b, m: (b * nt + m, 0))

    def kspec(j):
        return pl.BlockSpec((tq, aw), lambda b, m: (b * nt + jnp.maximum(m - 2 + j, 0), 0))

    return pl.pallas_call(
        _chunk_attn_kernel,
        grid=(batch, nt),
        in_specs=[qspec, kspec(0), kspec(1), kspec(2), kspec(0), kspec(1), kspec(2),
                  _const_spec(bias.shape), _const_spec((1, aw))],
        out_specs=qspec,
        out_shape=jax.ShapeDtypeStruct((T, aw), BF16),
        compiler_params=_params("parallel", "parallel"),
    )(q, k, k, k, v, v, v, bias, g_norm)


def _mem_kv_kernel(m_ref, g_ref, w_ref, k_ref, v_ref):
    mn = _rms(m_ref[...], g_ref[...]).astype(BF16)
    kv = jnp.dot(mn, w_ref[...], preferred_element_type=F32)
    d = k_ref.shape[-1]
    k_ref[...] = kv[:, :d].astype(BF16)
    v_ref[...] = kv[:, d:].astype(BF16)


def _mem_kv(mem2d, batch, g, w_kv):
    R, D = mem2d.shape
    ml = R // batch
    row = pl.BlockSpec((ml, D), lambda b: (b, 0))
    return pl.pallas_call(
        _mem_kv_kernel,
        grid=(batch,),
        in_specs=[row, _const_spec((1, D)), _const_spec(w_kv.shape)],
        out_specs=[row, row],
        out_shape=[jax.ShapeDtypeStruct((R, D), BF16)] * 2,
        compiler_params=_params("parallel"),
    )(mem2d, g, w_kv)


def _mix_cross_kernel(x_ref, ya_ref, yb_ref, woa_ref, wob_ref, gc_ref, wq_ref, km_ref,
                      vm_ref, wo_ref, gf_ref, wqp_ref, x2_ref, hn_ref, qp_ref):
    x1 = (x_ref[...]
          + jnp.dot(ya_ref[...], woa_ref[...], preferred_element_type=F32)
          + jnp.dot(yb_ref[...], wob_ref[...], preferred_element_type=F32))
    hn2 = _rms(x1, gc_ref[...]).astype(BF16)
    q = jnp.dot(hn2, wq_ref[...], preferred_element_type=F32).astype(BF16)
    d = q.shape[-1]
    dh = d // MEM_HEADS
    scale = dh ** -0.5
    heads = []
    for h in range(MEM_HEADS):
        sl = slice(h * dh, (h + 1) * dh)
        s = lax.dot_general(q[:, sl], km_ref[:, sl], (((1,), (1,)), ((), ())),
                            preferred_element_type=F32) * scale
        mx = jnp.max(s, axis=-1, keepdims=True)
        p = jnp.exp(s - mx)
        l = jnp.sum(p, axis=-1, keepdims=True)
        o = jnp.dot(p.astype(BF16), vm_ref[:, sl], preferred_element_type=F32) * (1.0 / l)
        heads.append(o.astype(BF16))
    o_all = jnp.concatenate(heads, axis=-1)
    x2 = x1 + jnp.dot(o_all, wo_ref[...], preferred_element_type=F32)
    x2_ref[...] = x2
    hn3 = _rms(x2, gf_ref[...]).astype(BF16)
    hn_ref[...] = hn3
    qp_ref[...] = jnp.dot(hn3, wqp_ref[...], preferred_element_type=F32).astype(BF16)


def _mix_cross(x2d, ya, yb, wo_a, wo_b, g_cross, w_q, kmem, vmem, w_o, g_ffn, w_qp, batch):
    T, D = x2d.shape
    tm = TOKEN_TILE
    per_batch = T // batch // tm
    ml = kmem.shape[0] // batch
    row = lambda w: pl.BlockSpec((tm, w), lambda i: (i, 0))
    mem_spec = pl.BlockSpec((ml, D), lambda i: (i // per_batch, 0))
    qw = w_qp.shape[1]
    return pl.pallas_call(
        _mix_cross_kernel,
        grid=(T // tm,),
        in_specs=[row(D), row(ya.shape[1]), row(yb.shape[1]),
                  _const_spec(wo_a.shape), _const_spec(wo_b.shape), _const_spec((1, D)),
                  _const_spec(w_q.shape), mem_spec, mem_spec, _const_spec(w_o.shape),
                  _const_spec((1, D)), _const_spec(w_qp.shape)],
        out_specs=[row(D), row(D), row(qw)],
        out_shape=[jax.ShapeDtypeStruct((T, D), F32), jax.ShapeDtypeStruct((T, D), BF16),
                   jax.ShapeDtypeStruct((T, qw), BF16)],
        compiler_params=_params("parallel"),
    )(x2d, ya, yb, wo_a, wo_b, g_cross, w_q, kmem, vmem, w_o, g_ffn, w_qp)


def _vmax(a, b):
    if a is None:
        return b
    if b is None:
        return a
    return jnp.maximum(a, b)


def _vmin(a, b):
    if a is None or b is None:
        return None
    return jnp.minimum(a, b)


def _bitonic_merge_desc(vals):
    n = len(vals)
    vals = list(vals)
    j = n // 2
    while j >= 1:
        for i in range(n):
            l = i ^ j
            if l > i:
                hi, lo = _vmax(vals[i], vals[l]), _vmin(vals[i], vals[l])
                vals[i], vals[l] = hi, lo
        j //= 2
    return vals


def _bitonic_sort_desc(vals):
    n = len(vals)
    vals = list(vals)
    k = 2
    while k <= n:
        j = k // 2
        while j >= 1:
            for i in range(n):
                l = i ^ j
                if l > i:
                    hi, lo = _vmax(vals[i], vals[l]), _vmin(vals[i], vals[l])
                    if (i & k) == 0:
                        vals[i], vals[l] = hi, lo
                    else:
                        vals[i], vals[l] = lo, hi
            j //= 2
        k *= 2
    return vals


def _top16_desc(sc):
    n = sc.shape[-1]
    k = PEER_TOPK
    sc3 = sc.reshape(N_KEYS // SUBLANES, SUBLANES, n)
    vals = _bitonic_sort_desc([sc3[i] for i in range(k)])
    shift = SUBLANES // 2
    while shift >= 1:
        other = [pltpu.roll(v, shift, 0) for v in vals]
        vals = _bitonic_merge_desc([jnp.maximum(vals[i], other[k - 1 - i]) for i in range(k)])
        shift //= 2
    return vals


_CAND = [(a, b) for a in range(PEER_TOPK) for b in range(PEER_TOPK)
         if (a + 1) * (b + 1) <= PEER_TOPK]


def _peer_route_kernel(qp_ref, keys_ref, nsel_ref, c1_ref, rank2_ref, e2_ref):
    tn = qp_ref.shape[0]
    k = PEER_TOPK
    kd = keys_ref.shape[-1]
    sub = lax.broadcasted_iota(jnp.int32, (SUBLANES, tn), 0)

    scores = []
    tops = []
    for hp in range(2 * PEER_HEADS):
        qh = qp_ref[:, hp * kd:(hp + 1) * kd]
        sc = lax.dot_general(keys_ref[hp], qh, (((1,), (1,)), ((), ())),
                             preferred_element_type=F32)
        scores.append(sc)
        tops.append(_top16_desc(sc))

    def pack(p, a):
        out = tops[p][a]
        for h in range(1, PEER_HEADS):
            out = jnp.where(sub == h, tops[2 * h + p][a], out)
        return out

    V1 = [pack(0, a) for a in range(k)]
    V2 = [pack(1, b) for b in range(k)]
    cand = {ab: V1[ab[0]] + V2[ab[1]] for ab in _CAND}
    padded = [cand[ab] for ab in _CAND] + [None] * (64 - len(_CAND))
    tau = _bitonic_sort_desc(padded)[k - 1]
    E1 = [jnp.exp(V1[a] - V1[0]) for a in range(k)]
    E2 = [jnp.exp(V2[b] - V2[0]) for b in range(k)]
    zsum = jnp.zeros((SUBLANES, tn), F32)
    ncnt = [jnp.zeros((SUBLANES, tn), F32) for _ in range(k + 1)]
    for (a, b) in _CAND:
        hit = cand[(a, b)] >= tau
        zsum = zsum + jnp.where(hit, E1[a] * E2[b], 0.0)
        ncnt[a] = ncnt[a] + jnp.where(hit, 1.0, 0.0)
    zinv = 1.0 / zsum
    dn = [ncnt[a] - ncnt[a + 1] for a in range(k)]

    for h in range(PEER_HEADS):
        s1 = scores[2 * h]
        s2 = scores[2 * h + 1]
        row = lambda arr: arr[h:h + 1, :]
        nsel = jnp.zeros_like(s1)
        for a in range(k):
            nsel = nsel + jnp.where(s1 >= row(V1[a]), row(dn[a]), 0.0)
        rank2 = jnp.zeros_like(s2)
        for b in range(k):
            rank2 = rank2 + jnp.where(s2 < row(V2[b]), 1.0, 0.0)
        nsel_ref[:, h, :] = nsel
        c1_ref[:, h, :] = jnp.exp(s1 - row(V1[0])) * row(zinv)
        rank2_ref[h] = rank2.astype(rank2_ref.dtype)
        e2_ref[h] = jnp.exp(s2 - row(V2[0])).astype(e2_ref.dtype)


def _peer_route(qp, keys):
    T = qp.shape[0]
    tn = ROUTE_TILE
    i1_major = pl.BlockSpec((N_KEYS, PEER_HEADS, tn), lambda i: (0, 0, i))
    head_major = pl.BlockSpec((PEER_HEADS, N_KEYS, tn), lambda i: (0, 0, i))
    return pl.pallas_call(
        _peer_route_kernel,
        grid=(T // tn,),
        in_specs=[pl.BlockSpec((tn, qp.shape[1]), lambda i: (i, 0)), _const_spec(keys.shape)],
        out_specs=[i1_major, i1_major, head_major, head_major],
        out_shape=[jax.ShapeDtypeStruct((N_KEYS, PEER_HEADS, T), F32)] * 2
        + [jax.ShapeDtypeStruct((PEER_HEADS, N_KEYS, T), BF16)] * 2,
        compiler_params=_params("parallel"),
    )(qp, keys)


_GELU_C = float(np.sqrt(2.0 / np.pi))


def _gelu_tanh(x):
    log2e = float(np.log2(np.e))
    k1 = -2.0 * _GELU_C * log2e
    k3 = -2.0 * _GELU_C * 0.044715 * log2e
    e = jnp.exp2(x * (k1 + k3 * (x * x)))
    return x / (1.0 + e)


def _rows_bf16(row, n):
    packed = 2 * SUBLANES
    tile = jnp.broadcast_to(row, (packed, row.shape[-1])).astype(BF16)
    return jnp.tile(tile, (n // packed, 1))


def _peer_dense_kernel(hn_ref, u_ref, unext_ref, vt_ref, nsel_ref, c1_ref, rank2_ref, e2_ref,
                       x2_ref, gn_ref, o_ref, acc_ref, act_ref, *, final_norm):
    j = pl.program_id(1)
    te = u_ref.shape[0]

    @pl.when(j == 0)
    def _():
        acc_ref[...] = jnp.zeros_like(acc_ref)

    ch = act_ref.shape[1]
    groups = ch // N_KEYS
    nch = te // ch

    def up(rows):
        return lax.dot_general(rows, hn_ref[...], (((1,), (1,)), ((), ())),
                               preferred_element_type=F32)

    def rows_after(c):
        start = pl.multiple_of(jnp.minimum(c + 1, nch - 1) * ch, ch)
        return jnp.where(c + 1 < nch, u_ref[pl.ds(start, ch), :], unext_ref[...])

    def down(c, slot):
        act = act_ref[slot]
        hs = []
        for g in range(groups):
            i1 = j * (te // N_KEYS) + c * groups + g
            nsel = nsel_ref[i1]
            c1 = c1_ref[i1]
            gate = None
            for h in range(PEER_HEADS):
                n_b = _rows_bf16(nsel[h:h + 1, :], N_KEYS)
                c_b = _rows_bf16(c1[h:h + 1, :], N_KEYS)
                term = jnp.where(rank2_ref[h] < n_b, e2_ref[h], jnp.zeros_like(c_b)) * c_b
                gate = term if gate is None else gate + term
            es = slice(g * N_KEYS, (g + 1) * N_KEYS)
            hs.append(gate * _gelu_tanh(act[es, :]).astype(BF16))
        hmat = jnp.concatenate(hs, axis=0)
        acc_ref[...] += jnp.dot(vt_ref[c], hmat, preferred_element_type=F32)

    @pl.when(j == 0)
    def _():
        act_ref[0] = up(u_ref[0:ch, :])

    def pair(k, carry):
        for slot in range(2):
            c = 2 * k + slot
            act_ref[1 - slot] = up(rows_after(c))
            down(c, slot)
        return carry

    lax.fori_loop(0, nch // 2, pair, 0)

    @pl.when(j == pl.num_programs(1) - 1)
    def _():
        y = x2_ref[...] + acc_ref[...].T
        o_ref[...] = _rms(y, gn_ref[...]) if final_norm else y


def _peer_dense(hn3, u, vt, nsel, c1, rank2, e2, x2, g_final, final_norm):
    T, D = hn3.shape
    E = u.shape[0]
    tm, te, ch = DENSE_TOKEN_TILE, DENSE_EXPERT_TILE, DENSE_CHUNK
    i1_major = pl.BlockSpec((N_KEYS, PEER_HEADS, tm), lambda i, j: (0, 0, i))
    head_major = pl.BlockSpec((PEER_HEADS, N_KEYS, tm), lambda i, j: (0, 0, i))
    row = pl.BlockSpec((tm, D), lambda i, j: (i, 0))
    last = E // te - 1
    return pl.pallas_call(
        functools.partial(_peer_dense_kernel, final_norm=final_norm),
        grid=(T // tm, E // te),
        in_specs=[row, pl.BlockSpec((te, D), lambda i, j: (j, 0)),
                  pl.BlockSpec((ch, D), lambda i, j: (jnp.minimum(j + 1, last) * (te // ch), 0)),
                  pl.BlockSpec((te // ch, D, ch), lambda i, j: (j, 0, 0)),
                  i1_major, i1_major, head_major, head_major, row,
                  pl.BlockSpec((1, D), lambda i, j: (0, 0))],
        out_specs=row,
        out_shape=jax.ShapeDtypeStruct((T, D), F32),
        scratch_shapes=[pltpu.VMEM((D, tm), F32), pltpu.VMEM((2, ch, tm), F32)],
        compiler_params=_params("parallel", "arbitrary"),
    )(hn3, u, u, vt, nsel, c1, rank2, e2, x2, g_final)


def _block_diag(w):
    nb, bi, bo = w.shape
    eye = jnp.eye(nb, dtype=w.dtype)
    return (eye[:, None, :, None] * w[:, :, None, :]).reshape(nb * bi, nb * bo)


def kernel(x, mem, norm_mix, w_in, conv_w, conv_b, gate_a_w, gate_a_b, gate_x_w, gate_x_b, lru_lambda, rel_bias, norm_grp_a, norm_grp_b, w_out, norm_cross, norm_mem, w_q_mem, w_kv_mem, w_o_mem, norm_ffn, w_query, sub_keys, expert_u, expert_v, norm_final):
    B, S, D = x.shape
    depth = w_in.shape[0]
    lru_w = conv_w.shape[-1]
    att_w = (w_in.shape[-1] - 2 * lru_w) // 3
    row = lambda v: v.reshape(1, -1)

    cur = x.reshape(B * S, D)
    mem2d = mem.reshape(B * mem.shape[1], D)
    for l in range(depth):
        zl, q, k, v = _in_proj(cur, row(norm_mix[l]), w_in[l].astype(BF16), 2 * lru_w, att_w)
        w_gates = jnp.concatenate([_block_diag(gate_a_w[l]), _block_diag(gate_x_w[l])],
                                  axis=1).astype(BF16)
        b_gates = jnp.concatenate([gate_a_b[l], gate_x_b[l]]).reshape(1, -1)
        ya = _rg_lru(zl, B, conv_w[l], row(conv_b[l]), w_gates, b_gates,
                     row(lru_lambda[l]), row(norm_grp_a[l]))
        yb = _chunk_attn(q, k, v, B, _attention_bias(rel_bias[l]), row(norm_grp_b[l]))
        kmem, vmem = _mem_kv(mem2d, B, row(norm_mem[l]), w_kv_mem[l].astype(BF16))
        wo = w_out[l].astype(BF16)
        x2, hn3, qp = _mix_cross(cur, ya, yb, wo[:lru_w], wo[lru_w:], row(norm_cross[l]),
                                 w_q_mem[l].astype(BF16), kmem, vmem,
                                 w_o_mem[l].astype(BF16), row(norm_ffn[l]),
                                 w_query[l].astype(BF16), B)
        keys = sub_keys[l].reshape(2 * PEER_HEADS, N_KEYS, -1).astype(BF16)
        nsel, c1, rank2, e2 = _peer_route(qp, keys)
        n_exp = expert_v.shape[1]
        vt = expert_v[l].astype(BF16).reshape(n_exp // DENSE_CHUNK, DENSE_CHUNK, D)
        cur = _peer_dense(hn3, expert_u[l].astype(BF16), vt.transpose(0, 2, 1),
                          nsel, c1, rank2, e2, x2, row(norm_final), l == depth - 1)
    return cur.reshape(B, S, D)
```

```python
import functools

import jax
import jax.numpy as jnp
import numpy as np
from jax import lax
from jax.experimental import pallas as pl
from jax.experimental.pallas import tpu as pltpu

F32 = jnp.float32
BF16 = jnp.bfloat16

EPS = 1e-6
NEG_INF = -1e30

CHUNK = 64
LEFT_CHUNKS = 8
REL_CLIP = 128
CONV_WIDTH = 4
LRU_C = 8.0
ATT_HEADS = 8
MEM_HEADS = 4
PEER_HEADS = 8
N_KEYS = 128
PEER_TOPK = 16

SUBLANES = 8
LANES = 128
VMEM_LIMIT_BYTES = 56 * 1024 * 1024

TOKEN_TILE = 512
LRU_TILE = 256
ATT_TILE = 4 * CHUNK
ATT_WINDOW = ATT_TILE + LEFT_CHUNKS * CHUNK
ROUTE_TILE = 256
DENSE_TOKEN_TILE = 512
DENSE_EXPERT_TILE = 2048
DENSE_CHUNK = 512


def _params(*semantics, flags=None):
    return pltpu.CompilerParams(dimension_semantics=semantics,
                                vmem_limit_bytes=VMEM_LIMIT_BYTES, flags=flags)


def _rms(xf, g):
    return xf * lax.rsqrt(jnp.mean(xf * xf, axis=-1, keepdims=True) + EPS) * g


def _const_spec(shape):
    nd = len(shape)
    return pl.BlockSpec(shape, lambda *_: (0,) * nd)


def _in_proj_kernel(x_ref, g_ref, w_ref, zl_ref, q_ref, k_ref, v_ref):
    hn = _rms(x_ref[...], g_ref[...]).astype(BF16)
    z = jnp.dot(hn, w_ref[...], preferred_element_type=F32)
    lw = zl_ref.shape[-1]
    aw = q_ref.shape[-1]
    zl_ref[...] = z[:, :lw]
    q_ref[...] = z[:, lw:lw + aw].astype(BF16)
    k_ref[...] = z[:, lw + aw:lw + 2 * aw].astype(BF16)
    v_ref[...] = z[:, lw + 2 * aw:].astype(BF16)


def _in_proj(x2d, g, w_in, lru_w2, att_w):
    T, D = x2d.shape
    tm = TOKEN_TILE
    row = lambda w: pl.BlockSpec((tm, w), lambda i: (i, 0))
    return pl.pallas_call(
        _in_proj_kernel,
        grid=(T // tm,),
        in_specs=[row(D), _const_spec((1, D)), _const_spec(w_in.shape)],
        out_specs=[row(lru_w2), row(att_w), row(att_w), row(att_w)],
        out_shape=[jax.ShapeDtypeStruct((T, lru_w2), F32)]
        + [jax.ShapeDtypeStruct((T, att_w), BF16)] * 3,
        compiler_params=_params("parallel"),
    )(x2d, g, w_in)


def _rg_lru_kernel(z_ref, cw_ref, cb_ref, wg_ref, bg_ref, lam_ref, gn_ref, o_ref,
                   xbuf, hcar):
    ts, w = o_ref.shape
    hist = SUBLANES

    @pl.when(pl.program_id(1) == 0)
    def _():
        xbuf[0:hist, :] = jnp.zeros((hist, w), F32)
        hcar[...] = jnp.zeros_like(hcar)

    xl = z_ref[:, :w]
    gate = z_ref[:, w:]
    xbuf[hist:hist + ts, :] = xl
    xc = cb_ref[...] + cw_ref[CONV_WIDTH - 1:CONV_WIDTH, :] * xl
    for j in range(CONV_WIDTH - 1):
        off = hist - (CONV_WIDTH - 1) + j
        xc = xc + cw_ref[j:j + 1, :] * xbuf[off:off + ts, :]
    xbuf[0:hist, :] = xbuf[ts:ts + hist, :]

    gates = jnp.dot(xc.astype(BF16), wg_ref[...], preferred_element_type=F32) + bg_ref[...]
    r = jax.nn.sigmoid(gates[:, :w])
    ig = jax.nn.sigmoid(gates[:, w:])
    lam = lam_ref[...]
    sp = jnp.log1p(jnp.exp(-jnp.abs(lam))) + jnp.maximum(-lam, 0.0)
    log_a = (-LRU_C) * r * sp
    a = jnp.exp(log_a)
    mult = jnp.sqrt(jnp.maximum(1.0 - a * a, 0.0))
    b = mult * ig * xc

    rows = lax.broadcasted_iota(jnp.int32, (ts, w), 0)
    d = 1
    while d < ts:
        keep = rows >= d
        a_prev = jnp.where(keep, pltpu.roll(a, d, 0), 1.0)
        b_prev = jnp.where(keep, pltpu.roll(b, d, 0), 0.0)
        b = a * b_prev + b
        a = a * a_prev
        d *= 2
    h = b + a * hcar[0:1, :]
    hcar[...] = jnp.broadcast_to(h[ts - 1:ts, :], hcar.shape)

    y = h * jax.nn.gelu(gate)
    o_ref[...] = _rms(y, gn_ref[...]).astype(o_ref.dtype)


def _rg_lru(zl, batch, conv_w, conv_b, w_gates, b_gates, lam, g_norm):
    T, w2 = zl.shape
    w = w2 // 2
    ts = LRU_TILE
    nt = T // batch // ts
    return pl.pallas_call(
        _rg_lru_kernel,
        grid=(batch, nt),
        in_specs=[pl.BlockSpec((ts, w2), lambda b, s: (b * nt + s, 0)),
                  _const_spec(conv_w.shape), _const_spec((1, w)),
                  _const_spec(w_gates.shape), _const_spec((1, w2)),
                  _const_spec((1, w)), _const_spec((1, w))],
        out_specs=pl.BlockSpec((ts, w), lambda b, s: (b * nt + s, 0)),
        out_shape=jax.ShapeDtypeStruct((T, w), BF16),
        scratch_shapes=[pltpu.VMEM((ts + SUBLANES, w), F32), pltpu.VMEM((SUBLANES, w), F32)],
        compiler_params=_params("parallel", "arbitrary"),
    )(zl, conv_w, conv_b, w_gates, b_gates, lam, g_norm)


def _chunk_attn_kernel(q_ref, k0_ref, k1_ref, k2_ref, v0_ref, v1_ref, v2_ref, bias_ref,
                       gn_ref, o_ref):
    tq, aw = q_ref.shape
    m = pl.program_id(1)
    dh = aw // ATT_HEADS
    scale = dh ** -0.5
    kw = 3 * tq
    col_blk = lax.broadcasted_iota(jnp.int32, (tq, kw), 1) // tq
    valid = (col_blk + m) >= 2
    lane = lax.broadcasted_iota(jnp.int32, (1, LANES), 1)
    outs = []
    for pair in range(aw // LANES):
        sl = slice(pair * LANES, (pair + 1) * LANES)
        qp = q_ref[:, sl]
        kp = jnp.concatenate([k0_ref[:, sl], k1_ref[:, sl], k2_ref[:, sl]], axis=0)
        vp = jnp.concatenate([v0_ref[:, sl], v1_ref[:, sl], v2_ref[:, sl]], axis=0)
        acc = jnp.zeros((tq, LANES), F32)
        for e in range(LANES // dh):
            head = pair * (LANES // dh) + e
            sel = (lane >= e * dh) & (lane < (e + 1) * dh)
            qm = jnp.where(sel, qp, jnp.zeros_like(qp))
            vm = jnp.where(sel, vp, jnp.zeros_like(vp))
            s = lax.dot_general(qm, kp, (((1,), (1,)), ((), ())),
                                preferred_element_type=F32)
            s = jnp.where(valid, s * scale + bias_ref[head], NEG_INF)
            mx = jnp.max(s, axis=-1, keepdims=True)
            p = jnp.exp(s - mx)
            l = jnp.sum(p, axis=-1, keepdims=True)
            o = jnp.dot(p.astype(BF16), vm, preferred_element_type=F32)
            acc = acc + o * (1.0 / l)
        outs.append(acc)
    y = jnp.concatenate(outs, axis=-1)
    o_ref[...] = _rms(y, gn_ref[...]).astype(o_ref.dtype)


def _attention_bias(rel_bias):
    tq = ATT_TILE
    kw = 3 * tq
    qpos = np.arange(tq)[:, None]
    kpos = np.arange(kw)[None, :] - 2 * tq
    qc = qpos // CHUNK
    kc = np.floor_divide(kpos, CHUNK)
    band = (kc <= qc) & (kc >= qc - LEFT_CHUNKS)
    period = 4 * tq
    j = np.arange(period)
    col_minus_row = np.where(j < kw, j, j - period)
    idx = np.clip(2 * tq - col_minus_row, -REL_CLIP, REL_CLIP) + REL_CLIP
    profile = rel_bias[:, idx].astype(F32)
    heads = profile.shape[0]
    sheared = jnp.tile(profile, (1, tq))[:, :tq * (period - 1)].reshape(heads, tq, period - 1)
    return jnp.where(band[None], sheared[:, :, :kw], NEG_INF)


def _chunk_attn(q, k, v, batch, bias, g_norm):
    T, aw = q.shape
    tq = ATT_TILE
    nt = T // batch // tq
    qspec = pl.BlockSpec((tq, aw), lambda b, m: (b * nt + m, 0))

    def kspec(j):
        return pl.BlockSpec((tq, aw), lambda b, m: (b * nt + jnp.maximum(m - 2 + j, 0), 0))

    return pl.pallas_call(
        _chunk_attn_kernel,
        grid=(batch, nt),
        in_specs=[qspec, kspec(0), kspec(1), kspec(2), kspec(0), kspec(1), kspec(2),
                  _const_spec(bias.shape), _const_spec((1, aw))],
        out_specs=qspec,
        out_shape=jax.ShapeDtypeStruct((T, aw), BF16),
        compiler_params=_params("parallel", "parallel"),
    )(q, k, k, k, v, v, v, bias, g_norm)


def _mem_kv_kernel(m_ref, g_ref, w_ref, k_ref, v_ref):
    mn = _rms(m_ref[...], g_ref[...]).astype(BF16)
    kv = jnp.dot(mn, w_ref[...], preferred_element_type=F32)
    d = k_ref.shape[-1]
    k_ref[...] = kv[:, :d].astype(BF16)
    v_ref[...] = kv[:, d:].astype(BF16)


def _mem_kv(mem2d, batch, g, w_kv):
    R, D = mem2d.shape
    ml = R // batch
    row = pl.BlockSpec((ml, D), lambda b: (b, 0))
    return pl.pallas_call(
        _mem_kv_kernel,
        grid=(batch,),
        in_specs=[row, _const_spec((1, D)), _const_spec(w_kv.shape)],
        out_specs=[row, row],
        out_shape=[jax.ShapeDtypeStruct((R, D), BF16)] * 2,
        compiler_params=_params("parallel"),
    )(mem2d, g, w_kv)


def _mix_cross_kernel(x_ref, ya_ref, yb_ref, woa_ref, wob_ref, gc_ref, wq_ref, km_ref,
                      vm_ref, wo_ref, gf_ref, wqp_ref, x2_ref, hnt_ref, qp_ref):
    x1 = (x_ref[...]
          + jnp.dot(ya_ref[...], woa_ref[...], preferred_element_type=F32)
          + jnp.dot(yb_ref[...], wob_ref[...], preferred_element_type=F32))
    hn2 = _rms(x1, gc_ref[...]).astype(BF16)
    q = jnp.dot(hn2, wq_ref[...], preferred_element_type=F32).astype(BF16)
    d = q.shape[-1]
    dh = d // MEM_HEADS
    scale = dh ** -0.5
    heads = []
    for h in range(MEM_HEADS):
        sl = slice(h * dh, (h + 1) * dh)
        s = lax.dot_general(q[:, sl], km_ref[:, sl], (((1,), (1,)), ((), ())),
                            preferred_element_type=F32) * scale
        mx = jnp.max(s, axis=-1, keepdims=True)
        p = jnp.exp(s - mx)
        l = jnp.sum(p, axis=-1, keepdims=True)
        o = jnp.dot(p.astype(BF16), vm_ref[:, sl], preferred_element_type=F32) * (1.0 / l)
        heads.append(o.astype(BF16))
    o_all = jnp.concatenate(heads, axis=-1)
    x2 = x1 + jnp.dot(o_all, wo_ref[...], preferred_element_type=F32)
    x2_ref[...] = x2
    hn3 = _rms(x2, gf_ref[...])
    hnt_ref[...] = hn3.T.astype(BF16)
    qp_ref[...] = jnp.dot(hn3.astype(BF16), wqp_ref[...],
                          preferred_element_type=F32).astype(BF16)


def _mix_cross(x2d, ya, yb, wo_a, wo_b, g_cross, w_q, kmem, vmem, w_o, g_ffn, w_qp, batch):
    T, D = x2d.shape
    tm = TOKEN_TILE
    per_batch = T // batch // tm
    ml = kmem.shape[0] // batch
    row = lambda w: pl.BlockSpec((tm, w), lambda i: (i, 0))
    mem_spec = pl.BlockSpec((ml, D), lambda i: (i // per_batch, 0))
    qw = w_qp.shape[1]
    return pl.pallas_call(
        _mix_cross_kernel,
        grid=(T // tm,),
        in_specs=[row(D), row(ya.shape[1]), row(yb.shape[1]),
                  _const_spec(wo_a.shape), _const_spec(wo_b.shape), _const_spec((1, D)),
                  _const_spec(w_q.shape), mem_spec, mem_spec, _const_spec(w_o.shape),
                  _const_spec((1, D)), _const_spec(w_qp.shape)],
        out_specs=[row(D), pl.BlockSpec((D, tm), lambda i: (0, i)), row(qw)],
        out_shape=[jax.ShapeDtypeStruct((T, D), F32), jax.ShapeDtypeStruct((D, T), BF16),
                   jax.ShapeDtypeStruct((T, qw), BF16)],
        compiler_params=_params("parallel"),
    )(x2d, ya, yb, wo_a, wo_b, g_cross, w_q, kmem, vmem, w_o, g_ffn, w_qp)


def _vmax(a, b):
    if a is None:
        return b
    if b is None:
        return a
    return jnp.maximum(a, b)


def _vmin(a, b):
    if a is None or b is None:
        return None
    return jnp.minimum(a, b)


def _bitonic_merge_desc(vals):
    n = len(vals)
    vals = list(vals)
    j = n // 2
    while j >= 1:
        for i in range(n):
            l = i ^ j
            if l > i:
                hi, lo = _vmax(vals[i], vals[l]), _vmin(vals[i], vals[l])
                vals[i], vals[l] = hi, lo
        j //= 2
    return vals


def _bitonic_sort_desc(vals):
    n = len(vals)
    vals = list(vals)
    k = 2
    while k <= n:
        j = k // 2
        while j >= 1:
            for i in range(n):
                l = i ^ j
                if l > i:
                    hi, lo = _vmax(vals[i], vals[l]), _vmin(vals[i], vals[l])
                    if (i & k) == 0:
                        vals[i], vals[l] = hi, lo
                    else:
                        vals[i], vals[l] = lo, hi
            j //= 2
        k *= 2
    return vals


def _top16_desc(sc):
    n = sc.shape[-1]
    k = PEER_TOPK
    sc3 = sc.reshape(N_KEYS // SUBLANES, SUBLANES, n)
    vals = _bitonic_sort_desc([sc3[i] for i in range(k)])
    shift = SUBLANES // 2
    while shift >= 1:
        other = [pltpu.roll(v, shift, 0) for v in vals]
        vals = _bitonic_merge_desc([jnp.maximum(vals[i], other[k - 1 - i]) for i in range(k)])
        shift //= 2
    return vals


def _count_prefix16(test, rows):
    t8 = test(rows[7])
    t4 = test(jnp.where(t8, rows[11], rows[3]))
    lo = jnp.where(t4, rows[5], rows[1])
    hi = jnp.where(t4, rows[13], rows[9])
    t2 = test(jnp.where(t8, hi, lo))
    quad = [jnp.where(t2, rows[4 * q + 2], rows[4 * q]) for q in range(4)]
    lo = jnp.where(t4, quad[1], quad[0])
    hi = jnp.where(t4, quad[3], quad[2])
    t1 = test(jnp.where(t8, hi, lo))
    t16 = test(rows[15])
    count = jnp.where(t8, 8.0, 0.0) + jnp.where(t4, 4.0, 0.0)
    count = count + jnp.where(t2, 2.0, 0.0) + jnp.where(t1, 1.0, 0.0)
    return count + jnp.where(t16, 1.0, 0.0)


_CAND = [(a, b) for a in range(PEER_TOPK) for b in range(PEER_TOPK)
         if (a + 1) * (b + 1) <= PEER_TOPK]


def _peer_route_kernel(qp_ref, keys_ref, nsel_ref, c1_ref, rank2_ref, e2_ref):
    tn = qp_ref.shape[0]
    k = PEER_TOPK
    kd = keys_ref.shape[-1]
    sub = lax.broadcasted_iota(jnp.int32, (SUBLANES, tn), 0)

    scores = []
    tops = []
    for hp in range(2 * PEER_HEADS):
        qh = qp_ref[:, hp * kd:(hp + 1) * kd]
        sc = lax.dot_general(keys_ref[hp], qh, (((1,), (1,)), ((), ())),
                             preferred_element_type=F32)
        scores.append(sc)
        tops.append(_top16_desc(sc))

    def pack(p, a):
        out = tops[p][a]
        for h in range(1, PEER_HEADS):
            out = jnp.where(sub == h, tops[2 * h + p][a], out)
        return out

    V1 = [pack(0, a) for a in range(k)]
    V2 = [pack(1, b) for b in range(k)]
    cand = {ab: V1[ab[0]] + V2[ab[1]] for ab in _CAND}
    padded = [cand[ab] for ab in _CAND] + [None] * (64 - len(_CAND))
    tau = _bitonic_sort_desc(padded)[k - 1]
    E1 = [jnp.exp(V1[a] - V1[0]) for a in range(k)]
    E2 = [jnp.exp(V2[b] - V2[0]) for b in range(k)]
    zsum = jnp.zeros((SUBLANES, tn), F32)
    for (a, b) in _CAND:
        zsum = zsum + jnp.where(cand[(a, b)] >= tau, E1[a] * E2[b], 0.0)
    zinv = 1.0 / zsum

    for h in range(PEER_HEADS):
        s1 = scores[2 * h]
        s2 = scores[2 * h + 1]
        row = lambda arr: arr[h:h + 1, :]
        v2_rows = [row(V2[b]) for b in range(k)]
        tau_row = row(tau)
        nsel = _count_prefix16(lambda r: (s1 + r) >= tau_row, v2_rows)
        rank2 = _count_prefix16(lambda r: r > s2, v2_rows)
        nsel_ref[:, h, :] = nsel
        c1_ref[:, h, :] = jnp.exp(s1 - row(V1[0])) * row(zinv)
        rank2_ref[h] = rank2.astype(rank2_ref.dtype)
        e2_ref[h] = jnp.exp(s2 - row(V2[0])).astype(e2_ref.dtype)


def _peer_route(qp, keys):
    T = qp.shape[0]
    tn = ROUTE_TILE
    i1_major = pl.BlockSpec((N_KEYS, PEER_HEADS, tn), lambda i: (0, 0, i))
    head_major = pl.BlockSpec((PEER_HEADS, N_KEYS, tn), lambda i: (0, 0, i))
    return pl.pallas_call(
        _peer_route_kernel,
        grid=(T // tn,),
        in_specs=[pl.BlockSpec((tn, qp.shape[1]), lambda i: (i, 0)), _const_spec(keys.shape)],
        out_specs=[i1_major, i1_major, head_major, head_major],
        out_shape=[jax.ShapeDtypeStruct((N_KEYS, PEER_HEADS, T), F32)] * 2
        + [jax.ShapeDtypeStruct((PEER_HEADS, N_KEYS, T), BF16)] * 2,
        compiler_params=_params("parallel"),
    )(qp, keys)


_GELU_C = float(np.sqrt(2.0 / np.pi))


def _gelu_tanh(x):
    log2e = float(np.log2(np.e))
    k1 = -2.0 * _GELU_C * log2e
    k3 = -2.0 * _GELU_C * 0.044715 * log2e
    e = jnp.exp2(x * (k1 + k3 * (x * x)))
    return x / (1.0 + e)


def _rows_bf16(row, n):
    packed = 2 * SUBLANES
    tile = jnp.broadcast_to(row, (packed, row.shape[-1])).astype(BF16)
    return jnp.tile(tile, (n // packed, 1))


def _peer_dense_kernel(hnt_ref, u_ref, unext_ref, vt_ref, nsel_ref, c1_ref, rank2_ref, e2_ref,
                       x2_ref, gn_ref, o_ref, acc_ref, act_ref, *, final_norm):
    j = pl.program_id(1)
    te = u_ref.shape[0]

    @pl.when(j == 0)
    def _():
        acc_ref[...] = jnp.zeros_like(acc_ref)

    ch = act_ref.shape[1]
    groups = ch // N_KEYS
    nch = te // ch

    def up(rows):
        return jnp.dot(rows, hnt_ref[...], preferred_element_type=F32)

    def rows_after(c):
        start = pl.multiple_of(jnp.minimum(c + 1, nch - 1) * ch, ch)
        return jnp.where(c + 1 < nch, u_ref[pl.ds(start, ch), :], unext_ref[...])

    def down(c, slot):
        act = act_ref[slot]
        hs = []
        for g in range(groups):
            i1 = j * (te // N_KEYS) + c * groups + g
            nsel = nsel_ref[i1]
            c1 = c1_ref[i1]
            gate = None
            for h in range(PEER_HEADS):
                n_b = _rows_bf16(nsel[h:h + 1, :], N_KEYS)
                c_b = _rows_bf16(c1[h:h + 1, :], N_KEYS)
                term = jnp.where(rank2_ref[h] < n_b, e2_ref[h], jnp.zeros_like(c_b)) * c_b
                gate = term if gate is None else gate + term
            es = slice(g * N_KEYS, (g + 1) * N_KEYS)
            hs.append(gate * _gelu_tanh(act[es, :]).astype(BF16))
        hmat = jnp.concatenate(hs, axis=0)
        return jnp.dot(vt_ref[c], hmat, preferred_element_type=F32)

    @pl.when(j == 0)
    def _():
        act_ref[0] = up(u_ref[0:ch, :])

    def pair(k, carry):
        parts = []
        for slot in range(2):
            c = 2 * k + slot
            act_ref[1 - slot] = up(rows_after(c))
            parts.append(down(c, slot))
        acc_ref[...] += parts[0] + parts[1]
        return carry

    lax.fori_loop(0, nch // 2, pair, 0)

    @pl.when(j == pl.num_programs(1) - 1)
    def _():
        y = x2_ref[...] + acc_ref[...].T
        o_ref[...] = _rms(y, gn_ref[...]) if final_norm else y


def _peer_dense(hnt, u, vt, nsel, c1, rank2, e2, x2, g_final, final_norm):
    D, T = hnt.shape
    E = u.shape[0]
    tm, te, ch = DENSE_TOKEN_TILE, DENSE_EXPERT_TILE, DENSE_CHUNK
    i1_major = pl.BlockSpec((N_KEYS, PEER_HEADS, tm), lambda i, j: (0, 0, i))
    head_major = pl.BlockSpec((PEER_HEADS, N_KEYS, tm), lambda i, j: (0, 0, i))
    row = pl.BlockSpec((tm, D), lambda i, j: (i, 0))
    last = E // te - 1
    return pl.pallas_call(
        functools.partial(_peer_dense_kernel, final_norm=final_norm),
        grid=(T // tm, E // te),
        in_specs=[pl.BlockSpec((D, tm), lambda i, j: (0, i)),
                  pl.BlockSpec((te, D), lambda i, j: (j, 0)),
                  pl.BlockSpec((ch, D), lambda i, j: (jnp.minimum(j + 1, last) * (te // ch), 0)),
                  pl.BlockSpec((te // ch, D, ch), lambda i, j: (j, 0, 0)),
                  i1_major, i1_major, head_major, head_major, row,
                  pl.BlockSpec((1, D), lambda i, j: (0, 0))],
        out_specs=row,
        out_shape=jax.ShapeDtypeStruct((T, D), F32),
        scratch_shapes=[pltpu.VMEM((D, tm), F32), pltpu.VMEM((2, ch, tm), F32)],
        compiler_params=_params("parallel", "arbitrary"),
    )(hnt, u, u, vt, nsel, c1, rank2, e2, x2, g_final)


def _block_diag(w):
    nb, bi, bo = w.shape
    eye = jnp.eye(nb, dtype=w.dtype)
    return (eye[:, None, :, None] * w[:, :, None, :]).reshape(nb * bi, nb * bo)


def kernel(x, mem, norm_mix, w_in, conv_w, conv_b, gate_a_w, gate_a_b, gate_x_w, gate_x_b, lru_lambda, rel_bias, norm_grp_a, norm_grp_b, w_out, norm_cross, norm_mem, w_q_mem, w_kv_mem, w_o_mem, norm_ffn, w_query, sub_keys, expert_u, expert_v, norm_final):
    B, S, D = x.shape
    depth = w_in.shape[0]
    lru_w = conv_w.shape[-1]
    att_w = (w_in.shape[-1] - 2 * lru_w) // 3
    row = lambda v: v.reshape(1, -1)

    cur = x.reshape(B * S, D)
    mem2d = mem.reshape(B * mem.shape[1], D)
    for l in range(depth):
        zl, q, k, v = _in_proj(cur, row(norm_mix[l]), w_in[l].astype(BF16), 2 * lru_w, att_w)
        w_gates = jnp.concatenate([_block_diag(gate_a_w[l]), _block_diag(gate_x_w[l])],
                                  axis=1).astype(BF16)
        b_gates = jnp.concatenate([gate_a_b[l], gate_x_b[l]]).reshape(1, -1)
        ya = _rg_lru(zl, B, conv_w[l], row(conv_b[l]), w_gates, b_gates,
                     row(lru_lambda[l]), row(norm_grp_a[l]))
        yb = _chunk_attn(q, k, v, B, _attention_bias(rel_bias[l]), row(norm_grp_b[l]))
        kmem, vmem = _mem_kv(mem2d, B, row(norm_mem[l]), w_kv_mem[l].astype(BF16))
        wo = w_out[l].astype(BF16)
        x2, hn3, qp = _mix_cross(cur, ya, yb, wo[:lru_w], wo[lru_w:], row(norm_cross[l]),
                                 w_q_mem[l].astype(BF16), kmem, vmem,
                                 w_o_mem[l].astype(BF16), row(norm_ffn[l]),
                                 w_query[l].astype(BF16), B)
        keys = sub_keys[l].reshape(2 * PEER_HEADS, N_KEYS, -1).astype(BF16)
        nsel, c1, rank2, e2 = _peer_route(qp, keys)
        n_exp = expert_v.shape[1]
        vt = expert_v[l].astype(BF16).reshape(n_exp // DENSE_CHUNK, DENSE_CHUNK, D)
        cur = _peer_dense(hn3, expert_u[l].astype(BF16), vt.transpose(0, 2, 1),
                          nsel, c1, rank2, e2, x2, row(norm_final), l == depth - 1)
    return cur.reshape(B, S, D)
```

```python
import functools

import jax
import jax.numpy as jnp
import numpy as np
from jax import lax
from jax.experimental import pallas as pl
from jax.experimental.pallas import tpu as pltpu

F32 = jnp.float32
BF16 = jnp.bfloat16

EPS = 1e-6
NEG_INF = -1e30

CHUNK = 64
LEFT_CHUNKS = 8
REL_CLIP = 128
CONV_WIDTH = 4
LRU_C = 8.0
ATT_HEADS = 8
MEM_HEADS = 4
PEER_HEADS = 8
N_KEYS = 128
PEER_TOPK = 16

SUBLANES = 8
LANES = 128
VMEM_LIMIT_BYTES = 56 * 1024 * 1024

TOKEN_TILE = 512
LRU_TILE = 256
ATT_TILE = 4 * CHUNK
ATT_WINDOW = ATT_TILE + LEFT_CHUNKS * CHUNK
ROUTE_TILE = 256
DENSE_TOKEN_TILE = 512
DENSE_EXPERT_TILE = 2048
DENSE_CHUNK = 512


def _params(*semantics, flags=None):
    return pltpu.CompilerParams(dimension_semantics=semantics,
                                vmem_limit_bytes=VMEM_LIMIT_BYTES, flags=flags)


def _rms(xf, g):
    return xf * lax.rsqrt(jnp.mean(xf * xf, axis=-1, keepdims=True) + EPS) * g


def _const_spec(shape):
    nd = len(shape)
    return pl.BlockSpec(shape, lambda *_: (0,) * nd)


def _in_proj_kernel(x_ref, g_ref, w_ref, zl_ref, q_ref, k_ref, v_ref):
    hn = _rms(x_ref[...], g_ref[...]).astype(BF16)
    z = jnp.dot(hn, w_ref[...], preferred_element_type=F32)
    lw = zl_ref.shape[-1]
    aw = q_ref.shape[-1]
    zl_ref[...] = z[:, :lw]
    q_ref[...] = z[:, lw:lw + aw].astype(BF16)
    k_ref[...] = z[:, lw + aw:lw + 2 * aw].astype(BF16)
    v_ref[...] = z[:, lw + 2 * aw:].astype(BF16)


def _in_proj(x2d, g, w_in, lru_w2, att_w):
    T, D = x2d.shape
    tm = TOKEN_TILE
    row = lambda w: pl.BlockSpec((tm, w), lambda i: (i, 0))
    return pl.pallas_call(
        _in_proj_kernel,
        grid=(T // tm,),
        in_specs=[row(D), _const_spec((1, D)), _const_spec(w_in.shape)],
        out_specs=[row(lru_w2), row(att_w), row(att_w), row(att_w)],
        out_shape=[jax.ShapeDtypeStruct((T, lru_w2), F32)]
        + [jax.ShapeDtypeStruct((T, att_w), BF16)] * 3,
        compiler_params=_params("parallel"),
    )(x2d, g, w_in)


def _rg_lru_kernel(z_ref, cw_ref, cb_ref, wg_ref, bg_ref, lam_ref, gn_ref, o_ref,
                   xbuf, hcar):
    ts, w = o_ref.shape
    hist = SUBLANES

    @pl.when(pl.program_id(1) == 0)
    def _():
        xbuf[0:hist, :] = jnp.zeros((hist, w), F32)
        hcar[...] = jnp.zeros_like(hcar)

    xl = z_ref[:, :w]
    gate = z_ref[:, w:]
    xbuf[hist:hist + ts, :] = xl
    xc = cb_ref[...] + cw_ref[CONV_WIDTH - 1:CONV_WIDTH, :] * xl
    for j in range(CONV_WIDTH - 1):
        off = hist - (CONV_WIDTH - 1) + j
        xc = xc + cw_ref[j:j + 1, :] * xbuf[off:off + ts, :]
    xbuf[0:hist, :] = xbuf[ts:ts + hist, :]

    gates = jnp.dot(xc.astype(BF16), wg_ref[...], preferred_element_type=F32) + bg_ref[...]
    r = jax.nn.sigmoid(gates[:, :w])
    ig = jax.nn.sigmoid(gates[:, w:])
    lam = lam_ref[...]
    sp = jnp.log1p(jnp.exp(-jnp.abs(lam))) + jnp.maximum(-lam, 0.0)
    log_a = (-LRU_C) * r * sp
    a = jnp.exp(log_a)
    m2 = jnp.maximum(1.0 - a * a, 0.0)
    mult = jnp.where(m2 > 0.0, m2 * lax.rsqrt(m2), 0.0)
    b = mult * ig * xc

    rows = lax.broadcasted_iota(jnp.int32, (ts, w), 0) % SUBLANES
    d = 1
    while d < SUBLANES:
        keep = rows >= d
        a_prev = jnp.where(keep, pltpu.roll(a, d, 0), 1.0)
        b_prev = jnp.where(keep, pltpu.roll(b, d, 0), 0.0)
        b = a * b_prev + b
        a = a * a_prev
        d *= 2
    carry = hcar[0:1, :]
    groups = []
    for g in range(ts // SUBLANES):
        rs = slice(g * SUBLANES, (g + 1) * SUBLANES)
        hg = b[rs, :] + a[rs, :] * carry
        groups.append(hg)
        carry = hg[SUBLANES - 1:SUBLANES, :]
    h = jnp.concatenate(groups, axis=0)
    hcar[...] = jnp.broadcast_to(carry, hcar.shape)

    y = h * jax.nn.gelu(gate)
    o_ref[...] = _rms(y, gn_ref[...]).astype(o_ref.dtype)


def _rg_lru(zl, batch, conv_w, conv_b, w_gates, b_gates, lam, g_norm):
    T, w2 = zl.shape
    w = w2 // 2
    ts = LRU_TILE
    nt = T // batch // ts
    return pl.pallas_call(
        _rg_lru_kernel,
        grid=(batch, nt),
        in_specs=[pl.BlockSpec((ts, w2), lambda b, s: (b * nt + s, 0)),
                  _const_spec(conv_w.shape), _const_spec((1, w)),
                  _const_spec(w_gates.shape), _const_spec((1, w2)),
                  _const_spec((1, w)), _const_spec((1, w))],
        out_specs=pl.BlockSpec((ts, w), lambda b, s: (b * nt + s, 0)),
        out_shape=jax.ShapeDtypeStruct((T, w), BF16),
        scratch_shapes=[pltpu.VMEM((ts + SUBLANES, w), F32), pltpu.VMEM((SUBLANES, w), F32)],
        compiler_params=_params("parallel", "arbitrary"),
    )(zl, conv_w, conv_b, w_gates, b_gates, lam, g_norm)


def _chunk_attn_kernel(q_ref, k0_ref, k1_ref, k2_ref, v0_ref, v1_ref, v2_ref, bias_ref,
                       gn_ref, o_ref):
    tq, aw = q_ref.shape
    m = pl.program_id(1)
    dh = aw // ATT_HEADS
    scale = dh ** -0.5
    kw = 3 * tq
    col_blk = lax.broadcasted_iota(jnp.int32, (tq, kw), 1) // tq
    valid = (col_blk + m) >= 2
    lane = lax.broadcasted_iota(jnp.int32, (1, LANES), 1)
    outs = []
    for pair in range(aw // LANES):
        sl = slice(pair * LANES, (pair + 1) * LANES)
        qp = q_ref[:, sl]
        kp = jnp.concatenate([k0_ref[:, sl], k1_ref[:, sl], k2_ref[:, sl]], axis=0)
        vp = jnp.concatenate([v0_ref[:, sl], v1_ref[:, sl], v2_ref[:, sl]], axis=0)
        acc = jnp.zeros((tq, LANES), F32)
        for e in range(LANES // dh):
            head = pair * (LANES // dh) + e
            sel = (lane >= e * dh) & (lane < (e + 1) * dh)
            qm = jnp.where(sel, qp, jnp.zeros_like(qp))
            vm = jnp.where(sel, vp, jnp.zeros_like(vp))
            s = lax.dot_general(qm, kp, (((1,), (1,)), ((), ())),
                                preferred_element_type=F32)
            s = jnp.where(valid, s * scale + bias_ref[head], NEG_INF)
            mx = jnp.max(s, axis=-1, keepdims=True)
            p = jnp.exp(s - mx)
            l = jnp.sum(p, axis=-1, keepdims=True)
            o = jnp.dot(p.astype(BF16), vm, preferred_element_type=F32)
            acc = acc + o * (1.0 / l)
        outs.append(acc)
    y = jnp.concatenate(outs, axis=-1)
    o_ref[...] = _rms(y, gn_ref[...]).astype(o_ref.dtype)


def _attention_bias(rel_bias):
    tq = ATT_TILE
    kw = 3 * tq
    qpos = np.arange(tq)[:, None]
    kpos = np.arange(kw)[None, :] - 2 * tq
    qc = qpos // CHUNK
    kc = np.floor_divide(kpos, CHUNK)
    band = (kc <= qc) & (kc >= qc - LEFT_CHUNKS)
    period = 4 * tq
    j = np.arange(period)
    col_minus_row = np.where(j < kw, j, j - period)
    idx = np.clip(2 * tq - col_minus_row, -REL_CLIP, REL_CLIP) + REL_CLIP
    profile = rel_bias[:, idx].astype(F32)
    heads = profile.shape[0]
    sheared = jnp.tile(profile, (1, tq))[:, :tq * (period - 1)].reshape(heads, tq, period - 1)
    return jnp.where(band[None], sheared[:, :, :kw], NEG_INF)


def _chunk_attn(q, k, v, batch, bias, g_norm):
    T, aw = q.shape
    tq = ATT_TILE
    nt = T // batch // tq
    qspec = pl.BlockSpec((tq, aw), lambda b, m: (b * nt + m, 0))

    def kspec(j):
        return pl.BlockSpec((tq, aw), lambda b, m: (b * nt + jnp.maximum(m - 2 + j, 0), 0))

    return pl.pallas_call(
        _chunk_attn_kernel,
        grid=(batch, nt),
        in_specs=[qspec, kspec(0), kspec(1), kspec(2), kspec(0), kspec(1), kspec(2),
                  _const_spec(bias.shape), _const_spec((1, aw))],
        out_specs=qspec,
        out_shape=jax.ShapeDtypeStruct((T, aw), BF16),
        compiler_params=_params("parallel", "parallel"),
    )(q, k, k, k, v, v, v, bias, g_norm)


def _mem_kv_kernel(m_ref, g_ref, w_ref, k_ref, v_ref):
    mn = _rms(m_ref[...], g_ref[...]).astype(BF16)
    kv = jnp.dot(mn, w_ref[...], preferred_element_type=F32)
    d = k_ref.shape[-1]
    k_ref[...] = kv[:, :d].astype(BF16)
    v_ref[...] = kv[:, d:].astype(BF16)


def _mem_kv(mem2d, batch, g, w_kv):
    R, D = mem2d.shape
    ml = R // batch
    row = pl.BlockSpec((ml, D), lambda b: (b, 0))
    return pl.pallas_call(
        _mem_kv_kernel,
        grid=(batch,),
        in_specs=[row, _const_spec((1, D)), _const_spec(w_kv.shape)],
        out_specs=[row, row],
        out_shape=[jax.ShapeDtypeStruct((R, D), BF16)] * 2,
        compiler_params=_params("parallel"),
    )(mem2d, g, w_kv)


def _mix_cross_kernel(x_ref, ya_ref, yb_ref, woa_ref, wob_ref, gc_ref, wq_ref, km_ref,
                      vm_ref, wo_ref, gf_ref, wqp_ref, x2_ref, hnt_ref, qp_ref):
    x1 = (x_ref[...]
          + jnp.dot(ya_ref[...], woa_ref[...], preferred_element_type=F32)
          + jnp.dot(yb_ref[...], wob_ref[...], preferred_element_type=F32))
    hn2 = _rms(x1, gc_ref[...]).astype(BF16)
    q = jnp.dot(hn2, wq_ref[...], preferred_element_type=F32).astype(BF16)
    d = q.shape[-1]
    dh = d // MEM_HEADS
    scale = dh ** -0.5
    heads = []
    for h in range(MEM_HEADS):
        sl = slice(h * dh, (h + 1) * dh)
        s = lax.dot_general(q[:, sl], km_ref[:, sl], (((1,), (1,)), ((), ())),
                            preferred_element_type=F32) * scale
        mx = jnp.max(s, axis=-1, keepdims=True)
        p = jnp.exp(s - mx)
        l = jnp.sum(p, axis=-1, keepdims=True)
        o = jnp.dot(p.astype(BF16), vm_ref[:, sl], preferred_element_type=F32) * (1.0 / l)
        heads.append(o.astype(BF16))
    o_all = jnp.concatenate(heads, axis=-1)
    x2 = x1 + jnp.dot(o_all, wo_ref[...], preferred_element_type=F32)
    x2_ref[...] = x2
    hn3 = _rms(x2, gf_ref[...])
    hnt_ref[...] = hn3.T.astype(BF16)
    qp_ref[...] = jnp.dot(hn3.astype(BF16), wqp_ref[...],
                          preferred_element_type=F32).astype(BF16)


def _mix_cross(x2d, ya, yb, wo_a, wo_b, g_cross, w_q, kmem, vmem, w_o, g_ffn, w_qp, batch):
    T, D = x2d.shape
    tm = TOKEN_TILE
    per_batch = T // batch // tm
    ml = kmem.shape[0] // batch
    row = lambda w: pl.BlockSpec((tm, w), lambda i: (i, 0))
    mem_spec = pl.BlockSpec((ml, D), lambda i: (i // per_batch, 0))
    qw = w_qp.shape[1]
    return pl.pallas_call(
        _mix_cross_kernel,
        grid=(T // tm,),
        in_specs=[row(D), row(ya.shape[1]), row(yb.shape[1]),
                  _const_spec(wo_a.shape), _const_spec(wo_b.shape), _const_spec((1, D)),
                  _const_spec(w_q.shape), mem_spec, mem_spec, _const_spec(w_o.shape),
                  _const_spec((1, D)), _const_spec(w_qp.shape)],
        out_specs=[row(D), pl.BlockSpec((D, tm), lambda i: (0, i)), row(qw)],
        out_shape=[jax.ShapeDtypeStruct((T, D), F32), jax.ShapeDtypeStruct((D, T), BF16),
                   jax.ShapeDtypeStruct((T, qw), BF16)],
        compiler_params=_params("parallel"),
    )(x2d, ya, yb, wo_a, wo_b, g_cross, w_q, kmem, vmem, w_o, g_ffn, w_qp)


def _vmax(a, b):
    if a is None:
        return b
    if b is None:
        return a
    return jnp.maximum(a, b)


def _vmin(a, b):
    if a is None or b is None:
        return None
    return jnp.minimum(a, b)


def _bitonic_merge_desc(vals):
    n = len(vals)
    vals = list(vals)
    j = n // 2
    while j >= 1:
        for i in range(n):
            l = i ^ j
            if l > i:
                hi, lo = _vmax(vals[i], vals[l]), _vmin(vals[i], vals[l])
                vals[i], vals[l] = hi, lo
        j //= 2
    return vals


def _bitonic_sort_desc(vals):
    n = len(vals)
    vals = list(vals)
    k = 2
    while k <= n:
        j = k // 2
        while j >= 1:
            for i in range(n):
                l = i ^ j
                if l > i:
                    hi, lo = _vmax(vals[i], vals[l]), _vmin(vals[i], vals[l])
                    if (i & k) == 0:
                        vals[i], vals[l] = hi, lo
                    else:
                        vals[i], vals[l] = lo, hi
            j //= 2
        k *= 2
    return vals


def _top16_desc(sc):
    n = sc.shape[-1]
    k = PEER_TOPK
    sc3 = sc.reshape(N_KEYS // SUBLANES, SUBLANES, n)
    vals = _bitonic_sort_desc([sc3[i] for i in range(k)])
    shift = SUBLANES // 2
    while shift >= 1:
        other = [pltpu.roll(v, shift, 0) for v in vals]
        vals = _bitonic_merge_desc([jnp.maximum(vals[i], other[k - 1 - i]) for i in range(k)])
        shift //= 2
    return vals


def _count_prefix16(test, rows):
    t8 = test(rows[7])
    t4 = test(jnp.where(t8, rows[11], rows[3]))
    lo = jnp.where(t4, rows[5], rows[1])
    hi = jnp.where(t4, rows[13], rows[9])
    t2 = test(jnp.where(t8, hi, lo))
    quad = [jnp.where(t2, rows[4 * q + 2], rows[4 * q]) for q in range(4)]
    lo = jnp.where(t4, quad[1], quad[0])
    hi = jnp.where(t4, quad[3], quad[2])
    t1 = test(jnp.where(t8, hi, lo))
    t16 = test(rows[15])
    count = jnp.where(t8, 8.0, 0.0) + jnp.where(t4, 4.0, 0.0)
    count = count + jnp.where(t2, 2.0, 0.0) + jnp.where(t1, 1.0, 0.0)
    return count + jnp.where(t16, 1.0, 0.0)


_CAND = [(a, b) for a in range(PEER_TOPK) for b in range(PEER_TOPK)
         if (a + 1) * (b + 1) <= PEER_TOPK]


def _peer_route_kernel(qp_ref, keys_ref, nsel_ref, c1_ref, rank2_ref, e2_ref):
    tn = qp_ref.shape[0]
    k = PEER_TOPK
    kd = keys_ref.shape[-1]
    sub = lax.broadcasted_iota(jnp.int32, (SUBLANES, tn), 0)

    scores = []
    tops = []
    for hp in range(2 * PEER_HEADS):
        qh = qp_ref[:, hp * kd:(hp + 1) * kd]
        sc = lax.dot_general(keys_ref[hp], qh, (((1,), (1,)), ((), ())),
                             preferred_element_type=F32)
        scores.append(sc)
        tops.append(_top16_desc(sc))

    def pack(p, a):
        out = tops[p][a]
        for h in range(1, PEER_HEADS):
            out = jnp.where(sub == h, tops[2 * h + p][a], out)
        return out

    V1 = [pack(0, a) for a in range(k)]
    V2 = [pack(1, b) for b in range(k)]
    cand = {ab: V1[ab[0]] + V2[ab[1]] for ab in _CAND}
    padded = [cand[ab] for ab in _CAND] + [None] * (64 - len(_CAND))
    tau = _bitonic_sort_desc(padded)[k - 1]
    E1 = [jnp.exp(V1[a] - V1[0]) for a in range(k)]
    E2 = [jnp.exp(V2[b] - V2[0]) for b in range(k)]
    zsum = jnp.zeros((SUBLANES, tn), F32)
    for (a, b) in _CAND:
        zsum = zsum + jnp.where(cand[(a, b)] >= tau, E1[a] * E2[b], 0.0)
    zinv = 1.0 / zsum

    for h in range(PEER_HEADS):
        s1 = scores[2 * h]
        s2 = scores[2 * h + 1]
        slab = 2 * SUBLANES
        row = lambda arr: jnp.broadcast_to(arr[h:h + 1, :], (slab, tn))
        v2_rows = [row(V2[b]) for b in range(k)]
        tau_row = row(tau)
        v1_max, z_inv = row(V1[0]), row(zinv)
        for r in range(N_KEYS // slab):
            rs = slice(r * slab, (r + 1) * slab)
            s1r, s2r = s1[rs, :], s2[rs, :]
            nsel = _count_prefix16(lambda t: (s1r + t) >= tau_row, v2_rows)
            rank2 = _count_prefix16(lambda t: t > s2r, v2_rows)
            nsel_ref[h, rs, :] = nsel
            c1_ref[h, rs, :] = jnp.exp(s1r - v1_max) * z_inv
            rank2_ref[h, rs, :] = rank2.astype(rank2_ref.dtype)
            e2_ref[h, rs, :] = jnp.exp(s2r - v2_rows[0]).astype(e2_ref.dtype)


def _peer_route(qp, keys):
    T = qp.shape[0]
    tn = ROUTE_TILE
    head_major = pl.BlockSpec((PEER_HEADS, N_KEYS, tn), lambda i: (0, 0, i))
    return pl.pallas_call(
        _peer_route_kernel,
        grid=(T // tn,),
        in_specs=[pl.BlockSpec((tn, qp.shape[1]), lambda i: (i, 0)), _const_spec(keys.shape)],
        out_specs=[head_major] * 4,
        out_shape=[jax.ShapeDtypeStruct((PEER_HEADS, N_KEYS, T), F32)] * 2
        + [jax.ShapeDtypeStruct((PEER_HEADS, N_KEYS, T), BF16)] * 2,
        compiler_params=_params("parallel"),
    )(qp, keys)


_GELU_C = float(np.sqrt(2.0 / np.pi))


def _gelu_tanh(x):
    log2e = float(np.log2(np.e))
    k1 = -2.0 * _GELU_C * log2e
    k3 = -2.0 * _GELU_C * 0.044715 * log2e
    e = jnp.exp2(x * (k1 + k3 * (x * x)))
    return x / (1.0 + e)


def _rows_bf16(row, n):
    packed = 2 * SUBLANES
    tile = jnp.broadcast_to(row, (packed, row.shape[-1])).astype(BF16)
    return jnp.tile(tile, (n // packed, 1))


def _peer_dense_kernel(hnt_ref, ufirst_ref, unext_ref, vt_ref, nsel_ref, c1_ref, rank2_ref,
                       e2_ref, x2_ref, gn_ref, o_ref, acc_ref, act_ref, *, final_norm):
    j = pl.program_id(1)
    te = unext_ref.shape[0]

    @pl.when(j == 0)
    def _():
        acc_ref[...] = jnp.zeros_like(acc_ref)

    ch = act_ref.shape[1]
    groups = ch // N_KEYS
    nch = te // ch

    def up(rows):
        return jnp.dot(rows, hnt_ref[...], preferred_element_type=F32)

    def rows_after(c):
        return unext_ref[pl.ds(pl.multiple_of(c * ch, ch), ch), :]

    def down(c, slot):
        act = act_ref[slot]
        hs = []
        for g in range(groups):
            i1 = j * (te // N_KEYS) + c * groups + g
            gate = None
            for h in range(PEER_HEADS):
                n_b = _rows_bf16(nsel_ref[h, pl.ds(i1, 1), :], N_KEYS)
                c_b = _rows_bf16(c1_ref[h, pl.ds(i1, 1), :], N_KEYS)
                term = jnp.where(rank2_ref[h] < n_b, e2_ref[h], jnp.zeros_like(c_b)) * c_b
                gate = term if gate is None else gate + term
            es = slice(g * N_KEYS, (g + 1) * N_KEYS)
            hs.append(gate * _gelu_tanh(act[es, :]).astype(BF16))
        hmat = jnp.concatenate(hs, axis=0)
        return jnp.dot(vt_ref[c], hmat, preferred_element_type=F32)

    @pl.when(j == 0)
    def _():
        act_ref[0] = up(ufirst_ref[...])

    def pair(k, carry):
        parts = []
        for slot in range(2):
            c = 2 * k + slot
            act_ref[1 - slot] = up(rows_after(c))
            parts.append(down(c, slot))
        acc_ref[...] += parts[0] + parts[1]
        return carry

    lax.fori_loop(0, nch // 2, pair, 0)

    @pl.when(j == pl.num_programs(1) - 1)
    def _():
        y = x2_ref[...] + acc_ref[...].T
        o_ref[...] = _rms(y, gn_ref[...]) if final_norm else y


def _peer_dense(hnt, u, vt, nsel, c1, rank2, e2, x2, g_final, final_norm):
    D, T = hnt.shape
    E = u.shape[0]
    tm, te, ch = DENSE_TOKEN_TILE, DENSE_EXPERT_TILE, DENSE_CHUNK
    head_major = pl.BlockSpec((PEER_HEADS, N_KEYS, tm), lambda i, j: (0, 0, i))
    row = pl.BlockSpec((tm, D), lambda i, j: (i, 0))
    u_rotated = jnp.concatenate([u[ch:], u[:ch]], axis=0)
    return pl.pallas_call(
        functools.partial(_peer_dense_kernel, final_norm=final_norm),
        grid=(T // tm, E // te),
        in_specs=[pl.BlockSpec((D, tm), lambda i, j: (0, i)),
                  pl.BlockSpec((ch, D), lambda i, j: (0, 0)),
                  pl.BlockSpec((te, D), lambda i, j: (j, 0)),
                  pl.BlockSpec((te // ch, D, ch), lambda i, j: (j, 0, 0)),
                  head_major, head_major, head_major, head_major, row,
                  pl.BlockSpec((1, D), lambda i, j: (0, 0))],
        out_specs=row,
        out_shape=jax.ShapeDtypeStruct((T, D), F32),
        scratch_shapes=[pltpu.VMEM((D, tm), F32), pltpu.VMEM((2, ch, tm), F32)],
        compiler_params=_params("parallel", "arbitrary"),
    )(hnt, u, u_rotated, vt, nsel, c1, rank2, e2, x2, g_final)


def _block_diag(w):
    nb, bi, bo = w.shape
    eye = jnp.eye(nb, dtype=w.dtype)
    return (eye[:, None, :, None] * w[:, :, None, :]).reshape(nb * bi, nb * bo)


def kernel(x, mem, norm_mix, w_in, conv_w, conv_b, gate_a_w, gate_a_b, gate_x_w, gate_x_b, lru_lambda, rel_bias, norm_grp_a, norm_grp_b, w_out, norm_cross, norm_mem, w_q_mem, w_kv_mem, w_o_mem, norm_ffn, w_query, sub_keys, expert_u, expert_v, norm_final):
    B, S, D = x.shape
    depth = w_in.shape[0]
    lru_w = conv_w.shape[-1]
    att_w = (w_in.shape[-1] - 2 * lru_w) // 3
    row = lambda v: v.reshape(1, -1)

    cur = x.reshape(B * S, D)
    mem2d = mem.reshape(B * mem.shape[1], D)
    for l in range(depth):
        zl, q, k, v = _in_proj(cur, row(norm_mix[l]), w_in[l].astype(BF16), 2 * lru_w, att_w)
        w_gates = jnp.concatenate([_block_diag(gate_a_w[l]), _block_diag(gate_x_w[l])],
                                  axis=1).astype(BF16)
        b_gates = jnp.concatenate([gate_a_b[l], gate_x_b[l]]).reshape(1, -1)
        ya = _rg_lru(zl, B, conv_w[l], row(conv_b[l]), w_gates, b_gates,
                     row(lru_lambda[l]), row(norm_grp_a[l]))
        yb = _chunk_attn(q, k, v, B, _attention_bias(rel_bias[l]), row(norm_grp_b[l]))
        kmem, vmem = _mem_kv(mem2d, B, row(norm_mem[l]), w_kv_mem[l].astype(BF16))
        wo = w_out[l].astype(BF16)
        x2, hn3, qp = _mix_cross(cur, ya, yb, wo[:lru_w], wo[lru_w:], row(norm_cross[l]),
                                 w_q_mem[l].astype(BF16), kmem, vmem,
                                 w_o_mem[l].astype(BF16), row(norm_ffn[l]),
                                 w_query[l].astype(BF16), B)
        keys = sub_keys[l].reshape(2 * PEER_HEADS, N_KEYS, -1).astype(BF16)
        nsel, c1, rank2, e2 = _peer_route(qp, keys)
        n_exp = expert_v.shape[1]
        vt = expert_v[l].astype(BF16).reshape(n_exp // DENSE_CHUNK, DENSE_CHUNK, D)
        cur = _peer_dense(hn3, expert_u[l].astype(BF16), vt.transpose(0, 2, 1),
                          nsel, c1, rank2, e2, x2, row(norm_final), l == depth - 1)
    return cur.reshape(B, S, D)
```

```python
import functools

import jax
import jax.numpy as jnp
import numpy as np
from jax import lax
from jax.experimental import pallas as pl
from jax.experimental.pallas import tpu as pltpu

F32 = jnp.float32
BF16 = jnp.bfloat16

EPS = 1e-6
NEG_INF = -1e30

CHUNK = 64
LEFT_CHUNKS = 8
REL_CLIP = 128
CONV_WIDTH = 4
LRU_C = 8.0
ATT_HEADS = 8
MEM_HEADS = 4
PEER_HEADS = 8
N_KEYS = 128
PEER_TOPK = 16

SUBLANES = 8
LANES = 128
VMEM_LIMIT_BYTES = 56 * 1024 * 1024

TOKEN_TILE = 512
LRU_TILE = 256
ATT_TILE = 4 * CHUNK
ATT_WINDOW = ATT_TILE + LEFT_CHUNKS * CHUNK
ROUTE_TILE = 256
DENSE_TOKEN_TILE = 512
DENSE_EXPERT_TILE = 2048
DENSE_CHUNK = 1024


def _params(*semantics, flags=None):
    return pltpu.CompilerParams(dimension_semantics=semantics,
                                vmem_limit_bytes=VMEM_LIMIT_BYTES, flags=flags)


def _rms(xf, g):
    return xf * lax.rsqrt(jnp.mean(xf * xf, axis=-1, keepdims=True) + EPS) * g


def _const_spec(shape):
    nd = len(shape)
    return pl.BlockSpec(shape, lambda *_: (0,) * nd)


def _in_proj_kernel(x_ref, g_ref, w_ref, zl_ref, q_ref, k_ref, v_ref):
    hn = _rms(x_ref[...], g_ref[...]).astype(BF16)
    z = jnp.dot(hn, w_ref[...], preferred_element_type=F32)
    lw = zl_ref.shape[-1]
    aw = q_ref.shape[-1]
    zl_ref[...] = z[:, :lw]
    q_ref[...] = z[:, lw:lw + aw].astype(BF16)
    k_ref[...] = z[:, lw + aw:lw + 2 * aw].astype(BF16)
    v_ref[...] = z[:, lw + 2 * aw:].astype(BF16)


def _in_proj(x2d, g, w_in, lru_w2, att_w):
    T, D = x2d.shape
    tm = TOKEN_TILE
    row = lambda w: pl.BlockSpec((tm, w), lambda i: (i, 0))
    return pl.pallas_call(
        _in_proj_kernel,
        grid=(T // tm,),
        in_specs=[row(D), _const_spec((1, D)), _const_spec(w_in.shape)],
        out_specs=[row(lru_w2), row(att_w), row(att_w), row(att_w)],
        out_shape=[jax.ShapeDtypeStruct((T, lru_w2), F32)]
        + [jax.ShapeDtypeStruct((T, att_w), BF16)] * 3,
        compiler_params=_params("parallel"),
    )(x2d, g, w_in)


def _rg_lru_kernel(z_ref, cw_ref, cb_ref, wg_ref, bg_ref, lam_ref, gn_ref, o_ref,
                   xbuf, hcar):
    ts, w = o_ref.shape
    hist = SUBLANES

    @pl.when(pl.program_id(1) == 0)
    def _():
        xbuf[0:hist, :] = jnp.zeros((hist, w), F32)
        hcar[...] = jnp.zeros_like(hcar)

    xl = z_ref[:, :w]
    gate = z_ref[:, w:]
    xbuf[hist:hist + ts, :] = xl
    xc = cb_ref[...] + cw_ref[CONV_WIDTH - 1:CONV_WIDTH, :] * xl
    for j in range(CONV_WIDTH - 1):
        off = hist - (CONV_WIDTH - 1) + j
        xc = xc + cw_ref[j:j + 1, :] * xbuf[off:off + ts, :]
    xbuf[0:hist, :] = xbuf[ts:ts + hist, :]

    gates = jnp.dot(xc.astype(BF16), wg_ref[...], preferred_element_type=F32) + bg_ref[...]
    r = jax.nn.sigmoid(gates[:, :w])
    ig = jax.nn.sigmoid(gates[:, w:])
    lam = lam_ref[...]
    sp = jnp.log1p(jnp.exp(-jnp.abs(lam))) + jnp.maximum(-lam, 0.0)
    log_a = (-LRU_C) * r * sp
    a = jnp.exp(log_a)
    m2 = jnp.maximum(1.0 - a * a, 0.0)
    mult = jnp.where(m2 > 0.0, m2 * lax.rsqrt(m2), 0.0)
    b = mult * ig * xc

    rows = lax.broadcasted_iota(jnp.int32, (ts, w), 0) % SUBLANES
    d = 1
    while d < SUBLANES:
        keep = rows >= d
        a_prev = jnp.where(keep, pltpu.roll(a, d, 0), 1.0)
        b_prev = jnp.where(keep, pltpu.roll(b, d, 0), 0.0)
        b = a * b_prev + b
        a = a * a_prev
        d *= 2
    carry = hcar[0:1, :]
    groups = []
    for g in range(ts // SUBLANES):
        rs = slice(g * SUBLANES, (g + 1) * SUBLANES)
        hg = b[rs, :] + a[rs, :] * carry
        groups.append(hg)
        carry = hg[SUBLANES - 1:SUBLANES, :]
    h = jnp.concatenate(groups, axis=0)
    hcar[...] = jnp.broadcast_to(carry, hcar.shape)

    y = h * jax.nn.gelu(gate)
    o_ref[...] = _rms(y, gn_ref[...]).astype(o_ref.dtype)


def _rg_lru(zl, batch, conv_w, conv_b, w_gates, b_gates, lam, g_norm):
    T, w2 = zl.shape
    w = w2 // 2
    ts = LRU_TILE
    nt = T // batch // ts
    return pl.pallas_call(
        _rg_lru_kernel,
        grid=(batch, nt),
        in_specs=[pl.BlockSpec((ts, w2), lambda b, s: (b * nt + s, 0)),
                  _const_spec(conv_w.shape), _const_spec((1, w)),
                  _const_spec(w_gates.shape), _const_spec((1, w2)),
                  _const_spec((1, w)), _const_spec((1, w))],
        out_specs=pl.BlockSpec((ts, w), lambda b, s: (b * nt + s, 0)),
        out_shape=jax.ShapeDtypeStruct((T, w), BF16),
        scratch_shapes=[pltpu.VMEM((ts + SUBLANES, w), F32), pltpu.VMEM((SUBLANES, w), F32)],
        compiler_params=_params("parallel", "arbitrary"),
    )(zl, conv_w, conv_b, w_gates, b_gates, lam, g_norm)


def _chunk_attn_kernel(q_ref, k0_ref, k1_ref, k2_ref, v0_ref, v1_ref, v2_ref, bias_ref,
                       gn_ref, o_ref):
    tq, aw = q_ref.shape
    m = pl.program_id(1)
    dh = aw // ATT_HEADS
    scale = dh ** -0.5
    kw = 3 * tq
    col_blk = lax.broadcasted_iota(jnp.int32, (tq, kw), 1) // tq
    valid = (col_blk + m) >= 2
    lane = lax.broadcasted_iota(jnp.int32, (1, LANES), 1)
    outs = []
    for pair in range(aw // LANES):
        sl = slice(pair * LANES, (pair + 1) * LANES)
        qp = q_ref[:, sl]
        kp = jnp.concatenate([k0_ref[:, sl], k1_ref[:, sl], k2_ref[:, sl]], axis=0)
        vp = jnp.concatenate([v0_ref[:, sl], v1_ref[:, sl], v2_ref[:, sl]], axis=0)
        acc = jnp.zeros((tq, LANES), F32)
        for e in range(LANES // dh):
            head = pair * (LANES // dh) + e
            sel = (lane >= e * dh) & (lane < (e + 1) * dh)
            qm = jnp.where(sel, qp, jnp.zeros_like(qp))
            vm = jnp.where(sel, vp, jnp.zeros_like(vp))
            s = lax.dot_general(qm, kp, (((1,), (1,)), ((), ())),
                                preferred_element_type=F32)
            s = jnp.where(valid, s * scale + bias_ref[head], NEG_INF)
            mx = jnp.max(s, axis=-1, keepdims=True)
            p = jnp.exp(s - mx)
            l = jnp.sum(p, axis=-1, keepdims=True)
            o = jnp.dot(p.astype(BF16), vm, preferred_element_type=F32)
            acc = acc + o * (1.0 / l)
        outs.append(acc)
    y = jnp.concatenate(outs, axis=-1)
    o_ref[...] = _rms(y, gn_ref[...]).astype(o_ref.dtype)


def _attention_bias(rel_bias):
    tq = ATT_TILE
    kw = 3 * tq
    qpos = np.arange(tq)[:, None]
    kpos = np.arange(kw)[None, :] - 2 * tq
    qc = qpos // CHUNK
    kc = np.floor_divide(kpos, CHUNK)
    band = (kc <= qc) & (kc >= qc - LEFT_CHUNKS)
    period = 4 * tq
    j = np.arange(period)
    col_minus_row = np.where(j < kw, j, j - period)
    idx = np.clip(2 * tq - col_minus_row, -REL_CLIP, REL_CLIP) + REL_CLIP
    profile = rel_bias[:, idx].astype(F32)
    heads = profile.shape[0]
    sheared = jnp.tile(profile, (1, tq))[:, :tq * (period - 1)].reshape(heads, tq, period - 1)
    return jnp.where(band[None], sheared[:, :, :kw], NEG_INF)


def _chunk_attn(q, k, v, batch, bias, g_norm):
    T, aw = q.shape
    tq = ATT_TILE
    nt = T // batch // tq
    qspec = pl.BlockSpec((tq, aw), lambda b, m: (b * nt + m, 0))

    def kspec(j):
        return pl.BlockSpec((tq, aw), lambda b, m: (b * nt + jnp.maximum(m - 2 + j, 0), 0))

    return pl.pallas_call(
        _chunk_attn_kernel,
        grid=(batch, nt),
        in_specs=[qspec, kspec(0), kspec(1), kspec(2), kspec(0), kspec(1), kspec(2),
                  _const_spec(bias.shape), _const_spec((1, aw))],
        out_specs=qspec,
        out_shape=jax.ShapeDtypeStruct((T, aw), BF16),
        compiler_params=_params("parallel", "parallel"),
    )(q, k, k, k, v, v, v, bias, g_norm)


def _mem_kv_kernel(m_ref, g_ref, w_ref, k_ref, v_ref):
    mn = _rms(m_ref[...], g_ref[...]).astype(BF16)
    kv = jnp.dot(mn, w_ref[...], preferred_element_type=F32)
    d = k_ref.shape[-1]
    k_ref[...] = kv[:, :d].astype(BF16)
    v_ref[...] = kv[:, d:].astype(BF16)


def _mem_kv(mem2d, batch, g, w_kv):
    R, D = mem2d.shape
    ml = R // batch
    row = pl.BlockSpec((ml, D), lambda b: (b, 0))
    return pl.pallas_call(
        _mem_kv_kernel,
        grid=(batch,),
        in_specs=[row, _const_spec((1, D)), _const_spec(w_kv.shape)],
        out_specs=[row, row],
        out_shape=[jax.ShapeDtypeStruct((R, D), BF16)] * 2,
        compiler_params=_params("parallel"),
    )(mem2d, g, w_kv)


def _mix_cross_kernel(x_ref, ya_ref, yb_ref, woa_ref, wob_ref, gc_ref, wq_ref, km_ref,
                      vm_ref, wo_ref, gf_ref, wqp_ref, x2_ref, hnt_ref, qp_ref):
    x1 = (x_ref[...]
          + jnp.dot(ya_ref[...], woa_ref[...], preferred_element_type=F32)
          + jnp.dot(yb_ref[...], wob_ref[...], preferred_element_type=F32))
    hn2 = _rms(x1, gc_ref[...]).astype(BF16)
    q = jnp.dot(hn2, wq_ref[...], preferred_element_type=F32).astype(BF16)
    d = q.shape[-1]
    dh = d // MEM_HEADS
    scale = dh ** -0.5
    heads = []
    for h in range(MEM_HEADS):
        sl = slice(h * dh, (h + 1) * dh)
        s = lax.dot_general(q[:, sl], km_ref[:, sl], (((1,), (1,)), ((), ())),
                            preferred_element_type=F32) * scale
        mx = jnp.max(s, axis=-1, keepdims=True)
        p = jnp.exp(s - mx)
        l = jnp.sum(p, axis=-1, keepdims=True)
        o = jnp.dot(p.astype(BF16), vm_ref[:, sl], preferred_element_type=F32) * (1.0 / l)
        heads.append(o.astype(BF16))
    o_all = jnp.concatenate(heads, axis=-1)
    x2 = x1 + jnp.dot(o_all, wo_ref[...], preferred_element_type=F32)
    x2_ref[...] = x2
    hn3 = _rms(x2, gf_ref[...])
    hnt_ref[...] = hn3.T.astype(BF16)
    qp_ref[...] = jnp.dot(hn3.astype(BF16), wqp_ref[...],
                          preferred_element_type=F32).astype(BF16)


def _mix_cross(x2d, ya, yb, wo_a, wo_b, g_cross, w_q, kmem, vmem, w_o, g_ffn, w_qp, batch):
    T, D = x2d.shape
    tm = TOKEN_TILE
    per_batch = T // batch // tm
    ml = kmem.shape[0] // batch
    row = lambda w: pl.BlockSpec((tm, w), lambda i: (i, 0))
    mem_spec = pl.BlockSpec((ml, D), lambda i: (i // per_batch, 0))
    qw = w_qp.shape[1]
    return pl.pallas_call(
        _mix_cross_kernel,
        grid=(T // tm,),
        in_specs=[row(D), row(ya.shape[1]), row(yb.shape[1]),
                  _const_spec(wo_a.shape), _const_spec(wo_b.shape), _const_spec((1, D)),
                  _const_spec(w_q.shape), mem_spec, mem_spec, _const_spec(w_o.shape),
                  _const_spec((1, D)), _const_spec(w_qp.shape)],
        out_specs=[row(D), pl.BlockSpec((D, tm), lambda i: (0, i)), row(qw)],
        out_shape=[jax.ShapeDtypeStruct((T, D), F32), jax.ShapeDtypeStruct((D, T), BF16),
                   jax.ShapeDtypeStruct((T, qw), BF16)],
        compiler_params=_params("parallel"),
    )(x2d, ya, yb, wo_a, wo_b, g_cross, w_q, kmem, vmem, w_o, g_ffn, w_qp)


def _vmax(a, b):
    if a is None:
        return b
    if b is None:
        return a
    return jnp.maximum(a, b)


def _vmin(a, b):
    if a is None or b is None:
        return None
    return jnp.minimum(a, b)


def _bitonic_merge_desc(vals):
    n = len(vals)
    vals = list(vals)
    j = n // 2
    while j >= 1:
        for i in range(n):
            l = i ^ j
            if l > i:
                hi, lo = _vmax(vals[i], vals[l]), _vmin(vals[i], vals[l])
                vals[i], vals[l] = hi, lo
        j //= 2
    return vals


def _bitonic_sort_desc(vals):
    n = len(vals)
    vals = list(vals)
    k = 2
    while k <= n:
        j = k // 2
        while j >= 1:
            for i in range(n):
                l = i ^ j
                if l > i:
                    hi, lo = _vmax(vals[i], vals[l]), _vmin(vals[i], vals[l])
                    if (i & k) == 0:
                        vals[i], vals[l] = hi, lo
                    else:
                        vals[i], vals[l] = lo, hi
            j //= 2
        k *= 2
    return vals


def _top16_desc(sc):
    n = sc.shape[-1]
    k = PEER_TOPK
    sc3 = sc.reshape(N_KEYS // SUBLANES, SUBLANES, n)
    vals = _bitonic_sort_desc([sc3[i] for i in range(k)])
    shift = SUBLANES // 2
    while shift >= 1:
        other = [pltpu.roll(v, shift, 0) for v in vals]
        vals = _bitonic_merge_desc([jnp.maximum(vals[i], other[k - 1 - i]) for i in range(k)])
        shift //= 2
    return vals


def _count_prefix16(test, rows):
    t8 = test(rows[7])
    t4 = test(jnp.where(t8, rows[11], rows[3]))
    lo = jnp.where(t4, rows[5], rows[1])
    hi = jnp.where(t4, rows[13], rows[9])
    t2 = test(jnp.where(t8, hi, lo))
    quad = [jnp.where(t2, rows[4 * q + 2], rows[4 * q]) for q in range(4)]
    lo = jnp.where(t4, quad[1], quad[0])
    hi = jnp.where(t4, quad[3], quad[2])
    t1 = test(jnp.where(t8, hi, lo))
    t16 = test(rows[15])
    count = jnp.where(t8, 8.0, 0.0) + jnp.where(t4, 4.0, 0.0)
    count = count + jnp.where(t2, 2.0, 0.0) + jnp.where(t1, 1.0, 0.0)
    return count + jnp.where(t16, 1.0, 0.0)


_CAND = [(a, b) for a in range(PEER_TOPK) for b in range(PEER_TOPK)
         if (a + 1) * (b + 1) <= PEER_TOPK]


def _peer_route_kernel(qp_ref, keys_ref, nsel_ref, c1_ref, rank2_ref, e2_ref):
    tn = qp_ref.shape[0]
    k = PEER_TOPK
    kd = keys_ref.shape[-1]
    sub = lax.broadcasted_iota(jnp.int32, (SUBLANES, tn), 0)

    scores = []
    tops = []
    for hp in range(2 * PEER_HEADS):
        qh = qp_ref[:, hp * kd:(hp + 1) * kd]
        sc = lax.dot_general(keys_ref[hp], qh, (((1,), (1,)), ((), ())),
                             preferred_element_type=F32)
        scores.append(sc)
        tops.append(_top16_desc(sc))

    def pack(p, a):
        out = tops[p][a]
        for h in range(1, PEER_HEADS):
            out = jnp.where(sub == h, tops[2 * h + p][a], out)
        return out

    V1 = [pack(0, a) for a in range(k)]
    V2 = [pack(1, b) for b in range(k)]
    cand = {ab: V1[ab[0]] + V2[ab[1]] for ab in _CAND}
    padded = [cand[ab] for ab in _CAND] + [None] * (64 - len(_CAND))
    tau = _bitonic_sort_desc(padded)[k - 1]
    E1 = [jnp.exp(V1[a] - V1[0]) for a in range(k)]
    E2 = [jnp.exp(V2[b] - V2[0]) for b in range(k)]
    zsum = jnp.zeros((SUBLANES, tn), F32)
    for (a, b) in _CAND:
        zsum = zsum + jnp.where(cand[(a, b)] >= tau, E1[a] * E2[b], 0.0)
    zinv = 1.0 / zsum

    for h in range(PEER_HEADS):
        s1 = scores[2 * h]
        s2 = scores[2 * h + 1]
        slab = 2 * SUBLANES
        row = lambda arr: jnp.broadcast_to(arr[h:h + 1, :], (slab, tn))
        v2_rows = [row(V2[b]) for b in range(k)]
        tau_row = row(tau)
        v1_max, z_inv = row(V1[0]), row(zinv)
        for r in range(N_KEYS // slab):
            rs = slice(r * slab, (r + 1) * slab)
            s1r, s2r = s1[rs, :], s2[rs, :]
            nsel = _count_prefix16(lambda t: (s1r + t) >= tau_row, v2_rows)
            rank2 = _count_prefix16(lambda t: t > s2r, v2_rows)
            nsel_ref[h, rs, :] = nsel
            c1_ref[h, rs, :] = jnp.exp(s1r - v1_max) * z_inv
            rank2_ref[h, rs, :] = rank2.astype(rank2_ref.dtype)
            e2_ref[h, rs, :] = jnp.exp(s2r - v2_rows[0]).astype(e2_ref.dtype)


def _peer_route(qp, keys):
    T = qp.shape[0]
    tn = ROUTE_TILE
    head_major = pl.BlockSpec((PEER_HEADS, N_KEYS, tn), lambda i: (0, 0, i))
    return pl.pallas_call(
        _peer_route_kernel,
        grid=(T // tn,),
        in_specs=[pl.BlockSpec((tn, qp.shape[1]), lambda i: (i, 0)), _const_spec(keys.shape)],
        out_specs=[head_major] * 4,
        out_shape=[jax.ShapeDtypeStruct((PEER_HEADS, N_KEYS, T), F32)] * 2
        + [jax.ShapeDtypeStruct((PEER_HEADS, N_KEYS, T), BF16)] * 2,
        compiler_params=_params("parallel"),
    )(qp, keys)


_GELU_C = float(np.sqrt(2.0 / np.pi))


def _gelu_tanh(x):
    log2e = float(np.log2(np.e))
    k1 = -2.0 * _GELU_C * log2e
    k3 = -2.0 * _GELU_C * 0.044715 * log2e
    e = jnp.exp2(x * (k1 + k3 * (x * x)))
    return x / (1.0 + e)


def _rows_bf16(row, n):
    packed = 2 * SUBLANES
    tile = jnp.broadcast_to(row, (packed, row.shape[-1])).astype(BF16)
    return jnp.tile(tile, (n // packed, 1))


def _peer_dense_kernel(hnt_ref, ufirst_ref, unext_ref, vt_ref, nsel_ref, c1_ref, rank2_ref,
                       e2_ref, x2_ref, gn_ref, o_ref, acc_ref, act_ref, *, final_norm):
    j = pl.program_id(1)
    te = unext_ref.shape[0]

    @pl.when(j == 0)
    def _():
        acc_ref[...] = jnp.zeros_like(acc_ref)

    ch = act_ref.shape[1]
    groups = ch // N_KEYS
    nch = te // ch

    def up(rows):
        return jnp.dot(rows, hnt_ref[...], preferred_element_type=F32)

    def rows_after(c):
        return unext_ref[pl.ds(pl.multiple_of(c * ch, ch), ch), :]

    def down(c, slot):
        act = act_ref[slot]
        hs = []
        for g in range(groups):
            i1 = j * (te // N_KEYS) + c * groups + g
            gate = None
            for h in range(PEER_HEADS):
                n_b = _rows_bf16(nsel_ref[h, pl.ds(i1, 1), :], N_KEYS)
                c_b = _rows_bf16(c1_ref[h, pl.ds(i1, 1), :], N_KEYS)
                term = jnp.where(rank2_ref[h] < n_b, e2_ref[h], jnp.zeros_like(c_b)) * c_b
                gate = term if gate is None else gate + term
            es = slice(g * N_KEYS, (g + 1) * N_KEYS)
            hs.append(gate * _gelu_tanh(act[es, :]).astype(BF16))
        hmat = jnp.concatenate(hs, axis=0)
        return jnp.dot(vt_ref[c], hmat, preferred_element_type=F32)

    @pl.when(j == 0)
    def _():
        act_ref[0] = up(ufirst_ref[...])

    def pair(k, carry):
        parts = []
        for slot in range(2):
            c = 2 * k + slot
            act_ref[1 - slot] = up(rows_after(c))
            parts.append(down(c, slot))
        acc_ref[...] += parts[0] + parts[1]
        return carry

    lax.fori_loop(0, nch // 2, pair, 0)

    @pl.when(j == pl.num_programs(1) - 1)
    def _():
        y = x2_ref[...] + acc_ref[...].T
        o_ref[...] = _rms(y, gn_ref[...]) if final_norm else y


def _peer_dense(hnt, u, v, nsel, c1, rank2, e2, x2, g_final, final_norm):
    D, T = hnt.shape
    E = u.shape[0]
    tm, te, ch = DENSE_TOKEN_TILE, DENSE_EXPERT_TILE, DENSE_CHUNK
    head_major = pl.BlockSpec((PEER_HEADS, N_KEYS, tm), lambda i, j: (0, 0, i))
    row = pl.BlockSpec((tm, D), lambda i, j: (i, 0))
    u_first = u[:ch].astype(BF16)
    u_rotated = jnp.concatenate([u[ch:], u[:ch]], axis=0).astype(BF16)
    vt = v.reshape(E // ch, ch, D).transpose(0, 2, 1).astype(BF16)
    return pl.pallas_call(
        functools.partial(_peer_dense_kernel, final_norm=final_norm),
        grid=(T // tm, E // te),
        in_specs=[pl.BlockSpec((D, tm), lambda i, j: (0, i)),
                  pl.BlockSpec((ch, D), lambda i, j: (0, 0)),
                  pl.BlockSpec((te, D), lambda i, j: (j, 0)),
                  pl.BlockSpec((te // ch, D, ch), lambda i, j: (j, 0, 0)),
                  head_major, head_major, head_major, head_major, row,
                  pl.BlockSpec((1, D), lambda i, j: (0, 0))],
        out_specs=row,
        out_shape=jax.ShapeDtypeStruct((T, D), F32),
        scratch_shapes=[pltpu.VMEM((D, tm), F32), pltpu.VMEM((2, ch, tm), F32)],
        compiler_params=_params("parallel", "arbitrary"),
    )(hnt, u_first, u_rotated, vt, nsel, c1, rank2, e2, x2, g_final)


def _block_diag(w):
    nb, bi, bo = w.shape
    eye = jnp.eye(nb, dtype=w.dtype)
    return (eye[:, None, :, None] * w[:, :, None, :]).reshape(nb * bi, nb * bo)


def kernel(x, mem, norm_mix, w_in, conv_w, conv_b, gate_a_w, gate_a_b, gate_x_w, gate_x_b, lru_lambda, rel_bias, norm_grp_a, norm_grp_b, w_out, norm_cross, norm_mem, w_q_mem, w_kv_mem, w_o_mem, norm_ffn, w_query, sub_keys, expert_u, expert_v, norm_final):
    B, S, D = x.shape
    depth = w_in.shape[0]
    lru_w = conv_w.shape[-1]
    att_w = (w_in.shape[-1] - 2 * lru_w) // 3
    row = lambda v: v.reshape(1, -1)

    cur = x.reshape(B * S, D)
    mem2d = mem.reshape(B * mem.shape[1], D)
    for l in range(depth):
        zl, q, k, v = _in_proj(cur, row(norm_mix[l]), w_in[l].astype(BF16), 2 * lru_w, att_w)
        w_gates = jnp.concatenate([_block_diag(gate_a_w[l]), _block_diag(gate_x_w[l])],
                                  axis=1).astype(BF16)
        b_gates = jnp.concatenate([gate_a_b[l], gate_x_b[l]]).reshape(1, -1)
        ya = _rg_lru(zl, B, conv_w[l], row(conv_b[l]), w_gates, b_gates,
                     row(lru_lambda[l]), row(norm_grp_a[l]))
        yb = _chunk_attn(q, k, v, B, _attention_bias(rel_bias[l]), row(norm_grp_b[l]))
        kmem, vmem = _mem_kv(mem2d, B, row(norm_mem[l]), w_kv_mem[l].astype(BF16))
        wo = w_out[l].astype(BF16)
        x2, hn3, qp = _mix_cross(cur, ya, yb, wo[:lru_w], wo[lru_w:], row(norm_cross[l]),
                                 w_q_mem[l].astype(BF16), kmem, vmem,
                                 w_o_mem[l].astype(BF16), row(norm_ffn[l]),
                                 w_query[l].astype(BF16), B)
        keys = sub_keys[l].reshape(2 * PEER_HEADS, N_KEYS, -1).astype(BF16)
        nsel, c1, rank2, e2 = _peer_route(qp, keys)
        cur = _peer_dense(hn3, expert_u[l], expert_v[l], nsel, c1, rank2, e2, x2,
                          row(norm_final), l == depth - 1)
    return cur.reshape(B, S, D)
```

```python
import functools

import jax
import jax.numpy as jnp
import numpy as np
from jax import lax
from jax.experimental import pallas as pl
from jax.experimental.pallas import tpu as pltpu

F32 = jnp.float32
BF16 = jnp.bfloat16

EPS = 1e-6
NEG_INF = -1e30

CHUNK = 64
LEFT_CHUNKS = 8
REL_CLIP = 128
CONV_WIDTH = 4
LRU_C = 8.0
ATT_HEADS = 8
MEM_HEADS = 4
PEER_HEADS = 8
N_KEYS = 128
PEER_TOPK = 16

SUBLANES = 8
LANES = 128
VMEM_LIMIT_BYTES = 56 * 1024 * 1024

TOKEN_TILE = 512
LRU_TILE = 256
ATT_TILE = 4 * CHUNK
ATT_WINDOW = ATT_TILE + LEFT_CHUNKS * CHUNK
ROUTE_TILE = 256
DENSE_TOKEN_TILE = 512
DENSE_EXPERT_TILE = 2048
DENSE_CHUNK = 512


def _params(*semantics, flags=None):
    return pltpu.CompilerParams(dimension_semantics=semantics,
                                vmem_limit_bytes=VMEM_LIMIT_BYTES, flags=flags)


def _rms(xf, g):
    return xf * lax.rsqrt(jnp.mean(xf * xf, axis=-1, keepdims=True) + EPS) * g


def _const_spec(shape):
    nd = len(shape)
    return pl.BlockSpec(shape, lambda *_: (0,) * nd)


def _in_proj_kernel(x_ref, g_ref, w_ref, zl_ref, q_ref, k_ref, v_ref):
    hn = _rms(x_ref[...], g_ref[...]).astype(BF16)
    z = jnp.dot(hn, w_ref[...], preferred_element_type=F32)
    lw = zl_ref.shape[-1]
    aw = q_ref.shape[-1]
    zl_ref[...] = z[:, :lw]
    q_ref[...] = z[:, lw:lw + aw].astype(BF16)
    k_ref[...] = z[:, lw + aw:lw + 2 * aw].astype(BF16)
    v_ref[...] = z[:, lw + 2 * aw:].astype(BF16)


def _in_proj(x2d, g, w_in, lru_w2, att_w):
    T, D = x2d.shape
    tm = TOKEN_TILE
    row = lambda w: pl.BlockSpec((tm, w), lambda i: (i, 0))
    return pl.pallas_call(
        _in_proj_kernel,
        grid=(T // tm,),
        in_specs=[row(D), _const_spec((1, D)), _const_spec(w_in.shape)],
        out_specs=[row(lru_w2), row(att_w), row(att_w), row(att_w)],
        out_shape=[jax.ShapeDtypeStruct((T, lru_w2), F32)]
        + [jax.ShapeDtypeStruct((T, att_w), BF16)] * 3,
        compiler_params=_params("parallel"),
    )(x2d, g, w_in)


def _rg_lru_kernel(z_ref, cw_ref, cb_ref, wg_ref, bg_ref, lam_ref, gn_ref, o_ref,
                   xbuf, hcar):
    ts, w = o_ref.shape
    hist = SUBLANES

    @pl.when(pl.program_id(1) == 0)
    def _():
        xbuf[0:hist, :] = jnp.zeros((hist, w), F32)
        hcar[...] = jnp.zeros_like(hcar)

    xl = z_ref[:, :w]
    gate = z_ref[:, w:]
    xbuf[hist:hist + ts, :] = xl
    xc = cb_ref[...] + cw_ref[CONV_WIDTH - 1:CONV_WIDTH, :] * xl
    for j in range(CONV_WIDTH - 1):
        off = hist - (CONV_WIDTH - 1) + j
        xc = xc + cw_ref[j:j + 1, :] * xbuf[off:off + ts, :]
    xbuf[0:hist, :] = xbuf[ts:ts + hist, :]

    gates = jnp.dot(xc.astype(BF16), wg_ref[...], preferred_element_type=F32) + bg_ref[...]
    r = jax.nn.sigmoid(gates[:, :w])
    ig = jax.nn.sigmoid(gates[:, w:])
    lam = lam_ref[...]
    sp = jnp.log1p(jnp.exp(-jnp.abs(lam))) + jnp.maximum(-lam, 0.0)
    log_a = (-LRU_C) * r * sp
    a = jnp.exp(log_a)
    m2 = jnp.maximum(1.0 - a * a, 0.0)
    mult = jnp.where(m2 > 0.0, m2 * lax.rsqrt(m2), 0.0)
    b = mult * ig * xc

    rows = lax.broadcasted_iota(jnp.int32, (ts, w), 0) % SUBLANES
    d = 1
    while d < SUBLANES:
        keep = rows >= d
        a_prev = jnp.where(keep, pltpu.roll(a, d, 0), 1.0)
        b_prev = jnp.where(keep, pltpu.roll(b, d, 0), 0.0)
        b = a * b_prev + b
        a = a * a_prev
        d *= 2
    carry = hcar[0:1, :]
    groups = []
    for g in range(ts // SUBLANES):
        rs = slice(g * SUBLANES, (g + 1) * SUBLANES)
        hg = b[rs, :] + a[rs, :] * carry
        groups.append(hg)
        carry = hg[SUBLANES - 1:SUBLANES, :]
    h = jnp.concatenate(groups, axis=0)
    hcar[...] = jnp.broadcast_to(carry, hcar.shape)

    y = h * jax.nn.gelu(gate)
    o_ref[...] = _rms(y, gn_ref[...]).astype(o_ref.dtype)


def _rg_lru(zl, batch, conv_w, conv_b, w_gates, b_gates, lam, g_norm):
    T, w2 = zl.shape
    w = w2 // 2
    ts = LRU_TILE
    nt = T // batch // ts
    return pl.pallas_call(
        _rg_lru_kernel,
        grid=(batch, nt),
        in_specs=[pl.BlockSpec((ts, w2), lambda b, s: (b * nt + s, 0)),
                  _const_spec(conv_w.shape), _const_spec((1, w)),
                  _const_spec(w_gates.shape), _const_spec((1, w2)),
                  _const_spec((1, w)), _const_spec((1, w))],
        out_specs=pl.BlockSpec((ts, w), lambda b, s: (b * nt + s, 0)),
        out_shape=jax.ShapeDtypeStruct((T, w), BF16),
        scratch_shapes=[pltpu.VMEM((ts + SUBLANES, w), F32), pltpu.VMEM((SUBLANES, w), F32)],
        compiler_params=_params("parallel", "arbitrary"),
    )(zl, conv_w, conv_b, w_gates, b_gates, lam, g_norm)


def _chunk_attn_kernel(q_ref, k0_ref, k1_ref, k2_ref, v0_ref, v1_ref, v2_ref, bias_ref,
                       gn_ref, o_ref):
    tq, aw = q_ref.shape
    m = pl.program_id(1)
    dh = aw // ATT_HEADS
    scale = dh ** -0.5
    kw = 3 * tq
    col_blk = lax.broadcasted_iota(jnp.int32, (tq, kw), 1) // tq
    valid = (col_blk + m) >= 2
    lane = lax.broadcasted_iota(jnp.int32, (1, LANES), 1)
    outs = []
    for pair in range(aw // LANES):
        sl = slice(pair * LANES, (pair + 1) * LANES)
        qp = q_ref[:, sl]
        kp = jnp.concatenate([k0_ref[:, sl], k1_ref[:, sl], k2_ref[:, sl]], axis=0)
        vp = jnp.concatenate([v0_ref[:, sl], v1_ref[:, sl], v2_ref[:, sl]], axis=0)
        acc = jnp.zeros((tq, LANES), F32)
        for e in range(LANES // dh):
            head = pair * (LANES // dh) + e
            sel = (lane >= e * dh) & (lane < (e + 1) * dh)
            qm = jnp.where(sel, qp, jnp.zeros_like(qp))
            vm = jnp.where(sel, vp, jnp.zeros_like(vp))
            s = lax.dot_general(qm, kp, (((1,), (1,)), ((), ())),
                                preferred_element_type=F32)
            s = jnp.where(valid, s * scale + bias_ref[head], NEG_INF)
            mx = jnp.max(s, axis=-1, keepdims=True)
            p = jnp.exp(s - mx)
            l = jnp.sum(p, axis=-1, keepdims=True)
            o = jnp.dot(p.astype(BF16), vm, preferred_element_type=F32)
            acc = acc + o * (1.0 / l)
        outs.append(acc)
    y = jnp.concatenate(outs, axis=-1)
    o_ref[...] = _rms(y, gn_ref[...]).astype(o_ref.dtype)


def _attention_bias(rel_bias):
    tq = ATT_TILE
    kw = 3 * tq
    qpos = np.arange(tq)[:, None]
    kpos = np.arange(kw)[None, :] - 2 * tq
    qc = qpos // CHUNK
    kc = np.floor_divide(kpos, CHUNK)
    band = (kc <= qc) & (kc >= qc - LEFT_CHUNKS)
    period = 4 * tq
    j = np.arange(period)
    col_minus_row = np.where(j < kw, j, j - period)
    idx = np.clip(2 * tq - col_minus_row, -REL_CLIP, REL_CLIP) + REL_CLIP
    profile = rel_bias[:, idx].astype(F32)
    heads = profile.shape[0]
    sheared = jnp.tile(profile, (1, tq))[:, :tq * (period - 1)].reshape(heads, tq, period - 1)
    return jnp.where(band[None], sheared[:, :, :kw], NEG_INF)


def _chunk_attn(q, k, v, batch, bias, g_norm):
    T, aw = q.shape
    tq = ATT_TILE
    nt = T // batch // tq
    qspec = pl.BlockSpec((tq, aw), lambda b, m: (b * nt + m, 0))

    def kspec(j):
        return pl.BlockSpec((tq, aw), lambda b, m: (b * nt + jnp.maximum(m - 2 + j, 0), 0))

    return pl.pallas_call(
        _chunk_attn_kernel,
        grid=(batch, nt),
        in_specs=[qspec, kspec(0), kspec(1), kspec(2), kspec(0), kspec(1), kspec(2),
                  _const_spec(bias.shape), _const_spec((1, aw))],
        out_specs=qspec,
        out_shape=jax.ShapeDtypeStruct((T, aw), BF16),
        compiler_params=_params("parallel", "parallel"),
    )(q, k, k, k, v, v, v, bias, g_norm)


def _mem_kv_kernel(m_ref, g_ref, w_ref, k_ref, v_ref):
    mn = _rms(m_ref[...], g_ref[...]).astype(BF16)
    kv = jnp.dot(mn, w_ref[...], preferred_element_type=F32)
    d = k_ref.shape[-1]
    k_ref[...] = kv[:, :d].astype(BF16)
    v_ref[...] = kv[:, d:].astype(BF16)


def _mem_kv(mem2d, batch, g, w_kv):
    R, D = mem2d.shape
    ml = R // batch
    row = pl.BlockSpec((ml, D), lambda b: (b, 0))
    return pl.pallas_call(
        _mem_kv_kernel,
        grid=(batch,),
        in_specs=[row, _const_spec((1, D)), _const_spec(w_kv.shape)],
        out_specs=[row, row],
        out_shape=[jax.ShapeDtypeStruct((R, D), BF16)] * 2,
        compiler_params=_params("parallel"),
    )(mem2d, g, w_kv)


def _mix_cross_kernel(x_ref, ya_ref, yb_ref, woa_ref, wob_ref, gc_ref, wq_ref, km_ref,
                      vm_ref, wo_ref, gf_ref, wqp_ref, x2_ref, hnt_ref, qp_ref):
    x1 = (x_ref[...]
          + jnp.dot(ya_ref[...], woa_ref[...], preferred_element_type=F32)
          + jnp.dot(yb_ref[...], wob_ref[...], preferred_element_type=F32))
    hn2 = _rms(x1, gc_ref[...]).astype(BF16)
    q = jnp.dot(hn2, wq_ref[...], preferred_element_type=F32).astype(BF16)
    d = q.shape[-1]
    dh = d // MEM_HEADS
    scale = dh ** -0.5
    heads = []
    for h in range(MEM_HEADS):
        sl = slice(h * dh, (h + 1) * dh)
        s = lax.dot_general(q[:, sl], km_ref[:, sl], (((1,), (1,)), ((), ())),
                            preferred_element_type=F32) * scale
        mx = jnp.max(s, axis=-1, keepdims=True)
        p = jnp.exp(s - mx)
        l = jnp.sum(p, axis=-1, keepdims=True)
        o = jnp.dot(p.astype(BF16), vm_ref[:, sl], preferred_element_type=F32) * (1.0 / l)
        heads.append(o.astype(BF16))
    o_all = jnp.concatenate(heads, axis=-1)
    x2 = x1 + jnp.dot(o_all, wo_ref[...], preferred_element_type=F32)
    x2_ref[...] = x2
    hn3 = _rms(x2, gf_ref[...])
    hnt_ref[...] = hn3.T.astype(BF16)
    qp_ref[...] = jnp.dot(hn3.astype(BF16), wqp_ref[...],
                          preferred_element_type=F32).astype(BF16)


def _mix_cross(x2d, ya, yb, wo_a, wo_b, g_cross, w_q, kmem, vmem, w_o, g_ffn, w_qp, batch):
    T, D = x2d.shape
    tm = TOKEN_TILE
    per_batch = T // batch // tm
    ml = kmem.shape[0] // batch
    row = lambda w: pl.BlockSpec((tm, w), lambda i: (i, 0))
    mem_spec = pl.BlockSpec((ml, D), lambda i: (i // per_batch, 0))
    qw = w_qp.shape[1]
    return pl.pallas_call(
        _mix_cross_kernel,
        grid=(T // tm,),
        in_specs=[row(D), row(ya.shape[1]), row(yb.shape[1]),
                  _const_spec(wo_a.shape), _const_spec(wo_b.shape), _const_spec((1, D)),
                  _const_spec(w_q.shape), mem_spec, mem_spec, _const_spec(w_o.shape),
                  _const_spec((1, D)), _const_spec(w_qp.shape)],
        out_specs=[row(D), pl.BlockSpec((D, tm), lambda i: (0, i)), row(qw)],
        out_shape=[jax.ShapeDtypeStruct((T, D), F32), jax.ShapeDtypeStruct((D, T), BF16),
                   jax.ShapeDtypeStruct((T, qw), BF16)],
        compiler_params=_params("parallel"),
    )(x2d, ya, yb, wo_a, wo_b, g_cross, w_q, kmem, vmem, w_o, g_ffn, w_qp)


def _vmax(a, b):
    if a is None:
        return b
    if b is None:
        return a
    return jnp.maximum(a, b)


def _vmin(a, b):
    if a is None or b is None:
        return None
    return jnp.minimum(a, b)


def _bitonic_merge_desc(vals):
    n = len(vals)
    vals = list(vals)
    j = n // 2
    while j >= 1:
        for i in range(n):
            l = i ^ j
            if l > i:
                hi, lo = _vmax(vals[i], vals[l]), _vmin(vals[i], vals[l])
                vals[i], vals[l] = hi, lo
        j //= 2
    return vals


def _bitonic_sort_desc(vals):
    n = len(vals)
    vals = list(vals)
    k = 2
    while k <= n:
        j = k // 2
        while j >= 1:
            for i in range(n):
                l = i ^ j
                if l > i:
                    hi, lo = _vmax(vals[i], vals[l]), _vmin(vals[i], vals[l])
                    if (i & k) == 0:
                        vals[i], vals[l] = hi, lo
                    else:
                        vals[i], vals[l] = lo, hi
            j //= 2
        k *= 2
    return vals


def _top16_desc(sc):
    n = sc.shape[-1]
    k = PEER_TOPK
    sc3 = sc.reshape(N_KEYS // SUBLANES, SUBLANES, n)
    vals = _bitonic_sort_desc([sc3[i] for i in range(k)])
    shift = SUBLANES // 2
    while shift >= 1:
        other = [pltpu.roll(v, shift, 0) for v in vals]
        vals = _bitonic_merge_desc([jnp.maximum(vals[i], other[k - 1 - i]) for i in range(k)])
        shift //= 2
    return vals


def _count_prefix16(test, rows):
    t8 = test(rows[7])
    t4 = test(jnp.where(t8, rows[11], rows[3]))
    lo = jnp.where(t4, rows[5], rows[1])
    hi = jnp.where(t4, rows[13], rows[9])
    t2 = test(jnp.where(t8, hi, lo))
    quad = [jnp.where(t2, rows[4 * q + 2], rows[4 * q]) for q in range(4)]
    lo = jnp.where(t4, quad[1], quad[0])
    hi = jnp.where(t4, quad[3], quad[2])
    t1 = test(jnp.where(t8, hi, lo))
    t16 = test(rows[15])
    count = jnp.where(t8, 8.0, 0.0) + jnp.where(t4, 4.0, 0.0)
    count = count + jnp.where(t2, 2.0, 0.0) + jnp.where(t1, 1.0, 0.0)
    return count + jnp.where(t16, 1.0, 0.0)


_CAND = [(a, b) for a in range(PEER_TOPK) for b in range(PEER_TOPK)
         if (a + 1) * (b + 1) <= PEER_TOPK]


def _peer_route_kernel(qp_ref, keys_ref, nsel_ref, c1_ref, rank2_ref, e2_ref):
    tn = qp_ref.shape[0]
    k = PEER_TOPK
    kd = keys_ref.shape[-1]
    sub = lax.broadcasted_iota(jnp.int32, (SUBLANES, tn), 0)

    scores = []
    tops = []
    for hp in range(2 * PEER_HEADS):
        qh = qp_ref[:, hp * kd:(hp + 1) * kd]
        sc = lax.dot_general(keys_ref[hp], qh, (((1,), (1,)), ((), ())),
                             preferred_element_type=F32)
        scores.append(sc)
        tops.append(_top16_desc(sc))

    def pack(p, a):
        out = tops[p][a]
        for h in range(1, PEER_HEADS):
            out = jnp.where(sub == h, tops[2 * h + p][a], out)
        return out

    V1 = [pack(0, a) for a in range(k)]
    V2 = [pack(1, b) for b in range(k)]
    cand = {ab: V1[ab[0]] + V2[ab[1]] for ab in _CAND}
    padded = [cand[ab] for ab in _CAND] + [None] * (64 - len(_CAND))
    tau = _bitonic_sort_desc(padded)[k - 1]
    E1 = [jnp.exp(V1[a] - V1[0]) for a in range(k)]
    E2 = [jnp.exp(V2[b] - V2[0]) for b in range(k)]
    zsum = jnp.zeros((SUBLANES, tn), F32)
    for (a, b) in _CAND:
        zsum = zsum + jnp.where(cand[(a, b)] >= tau, E1[a] * E2[b], 0.0)
    zinv = 1.0 / zsum

    for h in range(PEER_HEADS):
        s1 = scores[2 * h]
        s2 = scores[2 * h + 1]
        slab = 2 * SUBLANES
        row = lambda arr: jnp.broadcast_to(arr[h:h + 1, :], (slab, tn))
        v2_rows = [row(V2[b]) for b in range(k)]
        tau_row = row(tau)
        v1_max, z_inv = row(V1[0]), row(zinv)
        for r in range(N_KEYS // slab):
            rs = slice(r * slab, (r + 1) * slab)
            s1r, s2r = s1[rs, :], s2[rs, :]
            nsel = _count_prefix16(lambda t: (s1r + t) >= tau_row, v2_rows)
            rank2 = _count_prefix16(lambda t: t > s2r, v2_rows)
            nsel_ref[h, rs, :] = nsel
            c1_ref[h, rs, :] = jnp.exp(s1r - v1_max) * z_inv
            rank2_ref[h, rs, :] = rank2.astype(rank2_ref.dtype)
            e2_ref[h, rs, :] = jnp.exp(s2r - v2_rows[0]).astype(e2_ref.dtype)


def _peer_route(qp, keys):
    T = qp.shape[0]
    tn = ROUTE_TILE
    head_major = pl.BlockSpec((PEER_HEADS, N_KEYS, tn), lambda i: (0, 0, i))
    return pl.pallas_call(
        _peer_route_kernel,
        grid=(T // tn,),
        in_specs=[pl.BlockSpec((tn, qp.shape[1]), lambda i: (i, 0)), _const_spec(keys.shape)],
        out_specs=[head_major] * 4,
        out_shape=[jax.ShapeDtypeStruct((PEER_HEADS, N_KEYS, T), F32)] * 2
        + [jax.ShapeDtypeStruct((PEER_HEADS, N_KEYS, T), BF16)] * 2,
        compiler_params=_params("parallel"),
    )(qp, keys)


_GELU_C = float(np.sqrt(2.0 / np.pi))


def _gelu_tanh(x):
    log2e = float(np.log2(np.e))
    k1 = -2.0 * _GELU_C * log2e
    k3 = -2.0 * _GELU_C * 0.044715 * log2e
    e = jnp.exp2(x * (k1 + k3 * (x * x)))
    return x / (1.0 + e)


def _rows_bf16(row, n):
    packed = 2 * SUBLANES
    tile = jnp.broadcast_to(row, (packed, row.shape[-1])).astype(BF16)
    return jnp.tile(tile, (n // packed, 1))


def _peer_dense_kernel(hnt_ref, ufirst_ref, unext_ref, vt_ref, nsel_ref, c1_ref, rank2_ref,
                       e2_ref, x2_ref, gn_ref, o_ref, acc_ref, act_ref, *, final_norm):
    j = pl.program_id(1)
    te = unext_ref.shape[0]

    @pl.when(j == 0)
    def _():
        acc_ref[...] = jnp.zeros_like(acc_ref)

    ch = act_ref.shape[1]
    groups = ch // N_KEYS
    nch = te // ch

    def up(rows):
        return jnp.dot(rows, hnt_ref[...], preferred_element_type=F32)

    def rows_after(c):
        unshifted = (j == pl.num_programs(1) - 1).astype(jnp.int32)
        local = jnp.minimum(c + unshifted, nch - 1)
        return unext_ref[pl.ds(pl.multiple_of(local * ch, ch), ch), :]

    def down(c, slot):
        act = act_ref[slot]
        hs = []
        for g in range(groups):
            i1 = j * (te // N_KEYS) + c * groups + g
            gate = None
            for h in range(PEER_HEADS):
                n_b = _rows_bf16(nsel_ref[h, pl.ds(i1, 1), :], N_KEYS)
                c_b = _rows_bf16(c1_ref[h, pl.ds(i1, 1), :], N_KEYS)
                term = jnp.where(rank2_ref[h] < n_b, e2_ref[h], jnp.zeros_like(c_b)) * c_b
                gate = term if gate is None else gate + term
            es = slice(g * N_KEYS, (g + 1) * N_KEYS)
            hs.append(gate * _gelu_tanh(act[es, :]).astype(BF16))
        hmat = jnp.concatenate(hs, axis=0)
        return jnp.dot(vt_ref[c], hmat, preferred_element_type=F32)

    @pl.when(j == 0)
    def _():
        act_ref[0] = up(ufirst_ref[...])

    def pair(k, carry):
        parts = []
        for slot in range(2):
            c = 2 * k + slot
            act_ref[1 - slot] = up(rows_after(c))
            parts.append(down(c, slot))
        acc_ref[...] += parts[0] + parts[1]
        return carry

    lax.fori_loop(0, nch // 2, pair, 0)

    @pl.when(j == pl.num_programs(1) - 1)
    def _():
        y = x2_ref[...] + acc_ref[...].T
        o_ref[...] = _rms(y, gn_ref[...]) if final_norm else y


def _peer_dense(hnt, u, v, nsel, c1, rank2, e2, x2, g_final, final_norm):
    D, T = hnt.shape
    E = u.shape[0]
    tm, te, ch = DENSE_TOKEN_TILE, DENSE_EXPERT_TILE, DENSE_CHUNK
    head_major = pl.BlockSpec((PEER_HEADS, N_KEYS, tm), lambda i, j: (0, 0, i))
    row = pl.BlockSpec((tm, D), lambda i, j: (i, 0))
    u16 = u.astype(BF16)
    vt = v.astype(BF16).reshape(E // ch, ch, D).transpose(0, 2, 1)
    u_window = pl.BlockSpec((pl.Element(te), pl.Element(D)),
                            lambda i, j: (jnp.minimum(j * (te // ch) + 1, (E - te) // ch) * ch, 0))
    return pl.pallas_call(
        functools.partial(_peer_dense_kernel, final_norm=final_norm),
        grid=(T // tm, E // te),
        in_specs=[pl.BlockSpec((D, tm), lambda i, j: (0, i)),
                  pl.BlockSpec((ch, D), lambda i, j: (0, 0)),
                  u_window,
                  pl.BlockSpec((te // ch, D, ch), lambda i, j: (j, 0, 0)),
                  head_major, head_major, head_major, head_major, row,
                  pl.BlockSpec((1, D), lambda i, j: (0, 0))],
        out_specs=row,
        out_shape=jax.ShapeDtypeStruct((T, D), F32),
        scratch_shapes=[pltpu.VMEM((D, tm), F32), pltpu.VMEM((2, ch, tm), F32)],
        compiler_params=_params("parallel", "arbitrary"),
    )(hnt, u16, u16, vt, nsel, c1, rank2, e2, x2, g_final)


def _block_diag(w):
    nb, bi, bo = w.shape
    eye = jnp.eye(nb, dtype=w.dtype)
    return (eye[:, None, :, None] * w[:, :, None, :]).reshape(nb * bi, nb * bo)


def kernel(x, mem, norm_mix, w_in, conv_w, conv_b, gate_a_w, gate_a_b, gate_x_w, gate_x_b, lru_lambda, rel_bias, norm_grp_a, norm_grp_b, w_out, norm_cross, norm_mem, w_q_mem, w_kv_mem, w_o_mem, norm_ffn, w_query, sub_keys, expert_u, expert_v, norm_final):
    B, S, D = x.shape
    depth = w_in.shape[0]
    lru_w = conv_w.shape[-1]
    att_w = (w_in.shape[-1] - 2 * lru_w) // 3
    row = lambda v: v.reshape(1, -1)

    cur = x.reshape(B * S, D)
    mem2d = mem.reshape(B * mem.shape[1], D)
    for l in range(depth):
        zl, q, k, v = _in_proj(cur, row(norm_mix[l]), w_in[l].astype(BF16), 2 * lru_w, att_w)
        w_gates = jnp.concatenate([_block_diag(gate_a_w[l]), _block_diag(gate_x_w[l])],
                                  axis=1).astype(BF16)
        b_gates = jnp.concatenate([gate_a_b[l], gate_x_b[l]]).reshape(1, -1)
        ya = _rg_lru(zl, B, conv_w[l], row(conv_b[l]), w_gates, b_gates,
                     row(lru_lambda[l]), row(norm_grp_a[l]))
        yb = _chunk_attn(q, k, v, B, _attention_bias(rel_bias[l]), row(norm_grp_b[l]))
        kmem, vmem = _mem_kv(mem2d, B, row(norm_mem[l]), w_kv_mem[l].astype(BF16))
        wo = w_out[l].astype(BF16)
        x2, hn3, qp = _mix_cross(cur, ya, yb, wo[:lru_w], wo[lru_w:], row(norm_cross[l]),
                                 w_q_mem[l].astype(BF16), kmem, vmem,
                                 w_o_mem[l].astype(BF16), row(norm_ffn[l]),
                                 w_query[l].astype(BF16), B)
        keys = sub_keys[l].reshape(2 * PEER_HEADS, N_KEYS, -1).astype(BF16)
        nsel, c1, rank2, e2 = _peer_route(qp, keys)
        cur = _peer_dense(hn3, expert_u[l], expert_v[l], nsel, c1, rank2, e2, x2,
                          row(norm_final), l == depth - 1)
    return cur.reshape(B, S, D)
```

```python
import functools

import jax
import jax.numpy as jnp
import numpy as np
from jax import lax
from jax.experimental import pallas as pl
from jax.experimental.pallas import tpu as pltpu

F32 = jnp.float32
BF16 = jnp.bfloat16

EPS = 1e-6
NEG_INF = -1e30

CHUNK = 64
LEFT_CHUNKS = 8
REL_CLIP = 128
CONV_WIDTH = 4
LRU_C = 8.0
ATT_HEADS = 8
MEM_HEADS = 4
PEER_HEADS = 8
N_KEYS = 128
PEER_TOPK = 16

SUBLANES = 8
LANES = 128
VMEM_LIMIT_BYTES = 56 * 1024 * 1024

TOKEN_TILE = 512
LRU_TILE = 256
ATT_TILE = 4 * CHUNK
ATT_WINDOW = ATT_TILE + LEFT_CHUNKS * CHUNK
ROUTE_TILE = 256
DENSE_TOKEN_TILE = 512
DENSE_EXPERT_TILE = 2048
DENSE_CHUNK = 512


def _params(*semantics, flags=None):
    return pltpu.CompilerParams(dimension_semantics=semantics,
                                vmem_limit_bytes=VMEM_LIMIT_BYTES, flags=flags)


def _rms(xf, g):
    return xf * lax.rsqrt(jnp.mean(xf * xf, axis=-1, keepdims=True) + EPS) * g


def _const_spec(shape):
    nd = len(shape)
    return pl.BlockSpec(shape, lambda *_: (0,) * nd)


def _in_proj_kernel(x_ref, g_ref, w_ref, zl_ref, q_ref, k_ref, v_ref):
    hn = _rms(x_ref[...], g_ref[...]).astype(BF16)
    z = jnp.dot(hn, w_ref[...], preferred_element_type=F32)
    lw = zl_ref.shape[-1]
    aw = q_ref.shape[-1]
    zl_ref[...] = z[:, :lw]
    q_ref[...] = z[:, lw:lw + aw].astype(BF16)
    k_ref[...] = z[:, lw + aw:lw + 2 * aw].astype(BF16)
    v_ref[...] = z[:, lw + 2 * aw:].astype(BF16)


def _in_proj(x2d, g, w_in, lru_w2, att_w):
    T, D = x2d.shape
    tm = TOKEN_TILE
    row = lambda w: pl.BlockSpec((tm, w), lambda i: (i, 0))
    return pl.pallas_call(
        _in_proj_kernel,
        grid=(T // tm,),
        in_specs=[row(D), _const_spec((1, D)), _const_spec(w_in.shape)],
        out_specs=[row(lru_w2), row(att_w), row(att_w), row(att_w)],
        out_shape=[jax.ShapeDtypeStruct((T, lru_w2), F32)]
        + [jax.ShapeDtypeStruct((T, att_w), BF16)] * 3,
        compiler_params=_params("parallel"),
    )(x2d, g, w_in)


def _rg_lru_kernel(z_ref, cw_ref, cb_ref, wg_ref, bg_ref, lam_ref, gn_ref, o_ref,
                   xbuf, hcar):
    ts, w = o_ref.shape
    hist = SUBLANES

    @pl.when(pl.program_id(1) == 0)
    def _():
        xbuf[0:hist, :] = jnp.zeros((hist, w), F32)
        hcar[...] = jnp.zeros_like(hcar)

    xl = z_ref[:, :w]
    gate = z_ref[:, w:]
    xbuf[hist:hist + ts, :] = xl
    xc = cb_ref[...] + cw_ref[CONV_WIDTH - 1:CONV_WIDTH, :] * xl
    for j in range(CONV_WIDTH - 1):
        off = hist - (CONV_WIDTH - 1) + j
        xc = xc + cw_ref[j:j + 1, :] * xbuf[off:off + ts, :]
    xbuf[0:hist, :] = xbuf[ts:ts + hist, :]

    gates = jnp.dot(xc.astype(BF16), wg_ref[...], preferred_element_type=F32) + bg_ref[...]
    r = jax.nn.sigmoid(gates[:, :w])
    ig = jax.nn.sigmoid(gates[:, w:])
    lam = lam_ref[...]
    sp = jnp.log1p(jnp.exp(-jnp.abs(lam))) + jnp.maximum(-lam, 0.0)
    log_a = (-LRU_C) * r * sp
    a = jnp.exp(log_a)
    m2 = jnp.maximum(1.0 - a * a, 0.0)
    mult = jnp.where(m2 > 0.0, m2 * lax.rsqrt(m2), 0.0)
    b = mult * ig * xc

    rows = lax.broadcasted_iota(jnp.int32, (ts, w), 0) % SUBLANES
    d = 1
    while d < SUBLANES:
        keep = rows >= d
        a_prev = jnp.where(keep, pltpu.roll(a, d, 0), 1.0)
        b_prev = jnp.where(keep, pltpu.roll(b, d, 0), 0.0)
        b = a * b_prev + b
        a = a * a_prev
        d *= 2
    carry = hcar[0:1, :]
    groups = []
    for g in range(ts // SUBLANES):
        rs = slice(g * SUBLANES, (g + 1) * SUBLANES)
        hg = b[rs, :] + a[rs, :] * carry
        groups.append(hg)
        carry = hg[SUBLANES - 1:SUBLANES, :]
    h = jnp.concatenate(groups, axis=0)
    hcar[...] = jnp.broadcast_to(carry, hcar.shape)

    y = h * jax.nn.gelu(gate)
    o_ref[...] = _rms(y, gn_ref[...]).astype(o_ref.dtype)


def _rg_lru(zl, batch, conv_w, conv_b, w_gates, b_gates, lam, g_norm):
    T, w2 = zl.shape
    w = w2 // 2
    ts = LRU_TILE
    nt = T // batch // ts
    return pl.pallas_call(
        _rg_lru_kernel,
        grid=(batch, nt),
        in_specs=[pl.BlockSpec((ts, w2), lambda b, s: (b * nt + s, 0)),
                  _const_spec(conv_w.shape), _const_spec((1, w)),
                  _const_spec(w_gates.shape), _const_spec((1, w2)),
                  _const_spec((1, w)), _const_spec((1, w))],
        out_specs=pl.BlockSpec((ts, w), lambda b, s: (b * nt + s, 0)),
        out_shape=jax.ShapeDtypeStruct((T, w), BF16),
        scratch_shapes=[pltpu.VMEM((ts + SUBLANES, w), F32), pltpu.VMEM((SUBLANES, w), F32)],
        compiler_params=_params("parallel", "arbitrary"),
    )(zl, conv_w, conv_b, w_gates, b_gates, lam, g_norm)


def _chunk_attn_kernel(q_ref, k0_ref, k1_ref, k2_ref, v0_ref, v1_ref, v2_ref, bias_ref,
                       gn_ref, o_ref):
    tq, aw = q_ref.shape
    m = pl.program_id(1)
    dh = aw // ATT_HEADS
    scale = dh ** -0.5
    kw = 3 * tq
    assert LANES == 2 * dh
    col_blk = lax.broadcasted_iota(jnp.int32, (2 * tq, kw), 1) // tq
    valid = (col_blk + m) >= 2
    first = lax.broadcasted_iota(jnp.int32, (1, LANES), 1) < dh
    outs = []
    for pair in range(aw // LANES):
        sl = slice(pair * LANES, (pair + 1) * LANES)
        qp = q_ref[:, sl] * scale
        kp = jnp.concatenate([k0_ref[:, sl], k1_ref[:, sl], k2_ref[:, sl]], axis=0)
        vp = jnp.concatenate([v0_ref[:, sl], v1_ref[:, sl], v2_ref[:, sl]], axis=0)
        zero = jnp.zeros_like(qp)
        q2 = jnp.concatenate([jnp.where(first, qp, zero), jnp.where(first, zero, qp)], axis=0)
        s = lax.dot_general(q2, kp, (((1,), (1,)), ((), ())), preferred_element_type=F32)
        bias = bias_ref[2 * pair:2 * pair + 2].reshape(2 * tq, kw)
        s = jnp.where(valid, s + bias, NEG_INF)
        mx = jnp.max(s, axis=-1, keepdims=True)
        p = jnp.exp(s - mx).astype(BF16)
        ov = jnp.dot(p, jnp.concatenate([vp, jnp.ones_like(vp)], axis=1),
                     preferred_element_type=F32)
        o = ov[:, :LANES] * (1.0 / ov[:, LANES:LANES + 1])
        outs.append(jnp.where(first, o[:tq], o[tq:]))
    y = jnp.concatenate(outs, axis=-1)
    o_ref[...] = _rms(y, gn_ref[...]).astype(o_ref.dtype)


def _attention_bias(rel_bias):
    tq = ATT_TILE
    kw = 3 * tq
    qpos = np.arange(tq)[:, None]
    kpos = np.arange(kw)[None, :] - 2 * tq
    qc = qpos // CHUNK
    kc = np.floor_divide(kpos, CHUNK)
    band = (kc <= qc) & (kc >= qc - LEFT_CHUNKS)
    period = 4 * tq
    j = np.arange(period)
    col_minus_row = np.where(j < kw, j, j - period)
    idx = np.clip(2 * tq - col_minus_row, -REL_CLIP, REL_CLIP) + REL_CLIP
    profile = rel_bias[:, idx].astype(F32)
    heads = profile.shape[0]
    sheared = jnp.tile(profile, (1, tq))[:, :tq * (period - 1)].reshape(heads, tq, period - 1)
    return jnp.where(band[None], sheared[:, :, :kw], NEG_INF)


def _chunk_attn(q, k, v, batch, bias, g_norm):
    T, aw = q.shape
    tq = ATT_TILE
    nt = T // batch // tq
    qspec = pl.BlockSpec((tq, aw), lambda b, m: (b * nt + m, 0))

    def kspec(j):
        return pl.BlockSpec((tq, aw), lambda b, m: (b * nt + jnp.maximum(m - 2 + j, 0), 0))

    return pl.pallas_call(
        _chunk_attn_kernel,
        grid=(batch, nt),
        in_specs=[qspec, kspec(0), kspec(1), kspec(2), kspec(0), kspec(1), kspec(2),
                  _const_spec(bias.shape), _const_spec((1, aw))],
        out_specs=qspec,
        out_shape=jax.ShapeDtypeStruct((T, aw), BF16),
        compiler_params=_params("parallel", "parallel"),
    )(q, k, k, k, v, v, v, bias, g_norm)


def _mem_kv_kernel(m_ref, g_ref, w_ref, k_ref, v_ref):
    mn = _rms(m_ref[...], g_ref[...]).astype(BF16)
    kv = jnp.dot(mn, w_ref[...], preferred_element_type=F32)
    d = k_ref.shape[-1]
    k_ref[...] = kv[:, :d].astype(BF16)
    v_ref[...] = kv[:, d:].astype(BF16)


def _mem_kv(mem2d, batch, g, w_kv):
    R, D = mem2d.shape
    ml = R // batch
    row = pl.BlockSpec((ml, D), lambda b: (b, 0))
    return pl.pallas_call(
        _mem_kv_kernel,
        grid=(batch,),
        in_specs=[row, _const_spec((1, D)), _const_spec(w_kv.shape)],
        out_specs=[row, row],
        out_shape=[jax.ShapeDtypeStruct((R, D), BF16)] * 2,
        compiler_params=_params("parallel"),
    )(mem2d, g, w_kv)


def _mix_cross_kernel(x_ref, ya_ref, yb_ref, woa_ref, wob_ref, gc_ref, wq_ref, km_ref,
                      vm_ref, wo_ref, gf_ref, wqp_ref, x2_ref, hnt_ref, qp_ref):
    x1 = (x_ref[...]
          + jnp.dot(ya_ref[...], woa_ref[...], preferred_element_type=F32)
          + jnp.dot(yb_ref[...], wob_ref[...], preferred_element_type=F32))
    hn2 = _rms(x1, gc_ref[...]).astype(BF16)
    q = jnp.dot(hn2, wq_ref[...], preferred_element_type=F32).astype(BF16)
    d = q.shape[-1]
    dh = d // MEM_HEADS
    scale = dh ** -0.5
    heads = []
    for h in range(MEM_HEADS):
        sl = slice(h * dh, (h + 1) * dh)
        s = lax.dot_general(q[:, sl], km_ref[:, sl], (((1,), (1,)), ((), ())),
                            preferred_element_type=F32) * scale
        mx = jnp.max(s, axis=-1, keepdims=True)
        p = jnp.exp(s - mx)
        l = jnp.sum(p, axis=-1, keepdims=True)
        o = jnp.dot(p.astype(BF16), vm_ref[:, sl], preferred_element_type=F32) * (1.0 / l)
        heads.append(o.astype(BF16))
    o_all = jnp.concatenate(heads, axis=-1)
    x2 = x1 + jnp.dot(o_all, wo_ref[...], preferred_element_type=F32)
    x2_ref[...] = x2
    hn3 = _rms(x2, gf_ref[...])
    hnt_ref[...] = hn3.T.astype(BF16)
    qp_ref[...] = jnp.dot(hn3.astype(BF16), wqp_ref[...],
                          preferred_element_type=F32).astype(BF16)


def _mix_cross(x2d, ya, yb, wo_a, wo_b, g_cross, w_q, kmem, vmem, w_o, g_ffn, w_qp, batch):
    T, D = x2d.shape
    tm = TOKEN_TILE
    per_batch = T // batch // tm
    ml = kmem.shape[0] // batch
    row = lambda w: pl.BlockSpec((tm, w), lambda i: (i, 0))
    mem_spec = pl.BlockSpec((ml, D), lambda i: (i // per_batch, 0))
    qw = w_qp.shape[1]
    return pl.pallas_call(
        _mix_cross_kernel,
        grid=(T // tm,),
        in_specs=[row(D), row(ya.shape[1]), row(yb.shape[1]),
                  _const_spec(wo_a.shape), _const_spec(wo_b.shape), _const_spec((1, D)),
                  _const_spec(w_q.shape), mem_spec, mem_spec, _const_spec(w_o.shape),
                  _const_spec((1, D)), _const_spec(w_qp.shape)],
        out_specs=[row(D), pl.BlockSpec((D, tm), lambda i: (0, i)), row(qw)],
        out_shape=[jax.ShapeDtypeStruct((T, D), F32), jax.ShapeDtypeStruct((D, T), BF16),
                   jax.ShapeDtypeStruct((T, qw), BF16)],
        compiler_params=_params("parallel"),
    )(x2d, ya, yb, wo_a, wo_b, g_cross, w_q, kmem, vmem, w_o, g_ffn, w_qp)


def _vmax(a, b):
    if a is None:
        return b
    if b is None:
        return a
    return jnp.maximum(a, b)


def _vmin(a, b):
    if a is None or b is None:
        return None
    return jnp.minimum(a, b)


def _bitonic_merge_desc(vals):
    n = len(vals)
    vals = list(vals)
    j = n // 2
    while j >= 1:
        for i in range(n):
            l = i ^ j
            if l > i:
                hi, lo = _vmax(vals[i], vals[l]), _vmin(vals[i], vals[l])
                vals[i], vals[l] = hi, lo
        j //= 2
    return vals


def _bitonic_sort_desc(vals):
    n = len(vals)
    vals = list(vals)
    k = 2
    while k <= n:
        j = k // 2
        while j >= 1:
            for i in range(n):
                l = i ^ j
                if l > i:
                    hi, lo = _vmax(vals[i], vals[l]), _vmin(vals[i], vals[l])
                    if (i & k) == 0:
                        vals[i], vals[l] = hi, lo
                    else:
                        vals[i], vals[l] = lo, hi
            j //= 2
        k *= 2
    return vals


def _top16_desc(sc):
    n = sc.shape[-1]
    k = PEER_TOPK
    sc3 = sc.reshape(N_KEYS // SUBLANES, SUBLANES, n)
    vals = _bitonic_sort_desc([sc3[i] for i in range(k)])
    shift = SUBLANES // 2
    while shift >= 1:
        other = [pltpu.roll(v, shift, 0) for v in vals]
        vals = _bitonic_merge_desc([jnp.maximum(vals[i], other[k - 1 - i]) for i in range(k)])
        shift //= 2
    return vals


def _count_prefix16(test, rows):
    t8 = test(rows[7])
    t4 = test(jnp.where(t8, rows[11], rows[3]))
    lo = jnp.where(t4, rows[5], rows[1])
    hi = jnp.where(t4, rows[13], rows[9])
    t2 = test(jnp.where(t8, hi, lo))
    quad = [jnp.where(t2, rows[4 * q + 2], rows[4 * q]) for q in range(4)]
    lo = jnp.where(t4, quad[1], quad[0])
    hi = jnp.where(t4, quad[3], quad[2])
    t1 = test(jnp.where(t8, hi, lo))
    t16 = test(rows[15])
    count = jnp.where(t8, 8.0, 0.0) + jnp.where(t4, 4.0, 0.0)
    count = count + jnp.where(t2, 2.0, 0.0) + jnp.where(t1, 1.0, 0.0)
    return count + jnp.where(t16, 1.0, 0.0)


_CAND = [(a, b) for a in range(PEER_TOPK) for b in range(PEER_TOPK)
         if (a + 1) * (b + 1) <= PEER_TOPK]


def _peer_route_kernel(qp_ref, keys_ref, nsel_ref, c1_ref, rank2_ref, e2_ref):
    tn = qp_ref.shape[0]
    k = PEER_TOPK
    kd = keys_ref.shape[-1]
    sub = lax.broadcasted_iota(jnp.int32, (SUBLANES, tn), 0)

    scores = []
    tops = []
    for hp in range(2 * PEER_HEADS):
        qh = qp_ref[:, hp * kd:(hp + 1) * kd]
        sc = lax.dot_general(keys_ref[hp], qh, (((1,), (1,)), ((), ())),
                             preferred_element_type=F32)
        scores.append(sc)
        tops.append(_top16_desc(sc))

    def pack(p, a):
        out = tops[p][a]
        for h in range(1, PEER_HEADS):
            out = jnp.where(sub == h, tops[2 * h + p][a], out)
        return out

    V1 = [pack(0, a) for a in range(k)]
    V2 = [pack(1, b) for b in range(k)]
    cand = {ab: V1[ab[0]] + V2[ab[1]] for ab in _CAND}
    padded = [cand[ab] for ab in _CAND] + [None] * (64 - len(_CAND))
    tau = _bitonic_sort_desc(padded)[k - 1]
    E1 = [jnp.exp(V1[a] - V1[0]) for a in range(k)]
    E2 = [jnp.exp(V2[b] - V2[0]) for b in range(k)]
    zsum = jnp.zeros((SUBLANES, tn), F32)
    for (a, b) in _CAND:
        zsum = zsum + jnp.where(cand[(a, b)] >= tau, E1[a] * E2[b], 0.0)
    zinv = 1.0 / zsum

    for h in range(PEER_HEADS):
        s1 = scores[2 * h]
        s2 = scores[2 * h + 1]
        slab = 2 * SUBLANES
        row = lambda arr: jnp.broadcast_to(arr[h:h + 1, :], (slab, tn))
        v2_rows = [row(V2[b]) for b in range(k)]
        tau_row = row(tau)
        v1_max, z_inv = row(V1[0]), row(zinv)
        for r in range(N_KEYS // slab):
            rs = slice(r * slab, (r + 1) * slab)
            s1r, s2r = s1[rs, :], s2[rs, :]
            nsel = _count_prefix16(lambda t: (s1r + t) >= tau_row, v2_rows)
            rank2 = _count_prefix16(lambda t: t > s2r, v2_rows)
            nsel_ref[h, rs, :] = nsel
            c1_ref[h, rs, :] = jnp.exp(s1r - v1_max) * z_inv
            rank2_ref[h, rs, :] = rank2.astype(rank2_ref.dtype)
            e2_ref[h, rs, :] = jnp.exp(s2r - v2_rows[0]).astype(e2_ref.dtype)


def _peer_route(qp, keys):
    T = qp.shape[0]
    tn = ROUTE_TILE
    head_major = pl.BlockSpec((PEER_HEADS, N_KEYS, tn), lambda i: (0, 0, i))
    return pl.pallas_call(
        _peer_route_kernel,
        grid=(T // tn,),
        in_specs=[pl.BlockSpec((tn, qp.shape[1]), lambda i: (i, 0)), _const_spec(keys.shape)],
        out_specs=[head_major] * 4,
        out_shape=[jax.ShapeDtypeStruct((PEER_HEADS, N_KEYS, T), F32)] * 2
        + [jax.ShapeDtypeStruct((PEER_HEADS, N_KEYS, T), BF16)] * 2,
        compiler_params=_params("parallel"),
    )(qp, keys)


_GELU_C = float(np.sqrt(2.0 / np.pi))


def _gelu_tanh(x):
    log2e = float(np.log2(np.e))
    k1 = -2.0 * _GELU_C * log2e
    k3 = -2.0 * _GELU_C * 0.044715 * log2e
    e = jnp.exp2(x * (k1 + k3 * (x * x)))
    return x / (1.0 + e)


def _rows_bf16(row, n):
    packed = 2 * SUBLANES
    tile = jnp.broadcast_to(row, (packed, row.shape[-1])).astype(BF16)
    return jnp.tile(tile, (n // packed, 1))


def _peer_dense_kernel(hnt_ref, ufirst_ref, unext_ref, vt_ref, nsel_ref, c1_ref, rank2_ref,
                       e2_ref, x2_ref, gn_ref, o_ref, acc_ref, act_ref, *, final_norm):
    j = pl.program_id(1)
    te = unext_ref.shape[0]

    @pl.when(j == 0)
    def _():
        acc_ref[...] = jnp.zeros_like(acc_ref)

    ch = act_ref.shape[1]
    groups = ch // N_KEYS
    nch = te // ch

    def up(rows):
        return jnp.dot(rows, hnt_ref[...], preferred_element_type=F32)

    def rows_after(c):
        unshifted = (j == pl.num_programs(1) - 1).astype(jnp.int32)
        local = jnp.minimum(c + unshifted, nch - 1)
        return unext_ref[pl.ds(pl.multiple_of(local * ch, ch), ch), :]

    def down(c, slot):
        act = act_ref[slot]
        hs = []
        for g in range(groups):
            i1 = j * (te // N_KEYS) + c * groups + g
            gate = None
            for h in range(PEER_HEADS):
                n_b = _rows_bf16(nsel_ref[h, pl.ds(i1, 1), :], N_KEYS)
                c_b = _rows_bf16(c1_ref[h, pl.ds(i1, 1), :], N_KEYS)
                term = jnp.where(rank2_ref[h] < n_b, e2_ref[h], jnp.zeros_like(c_b)) * c_b
                gate = term if gate is None else gate + term
            es = slice(g * N_KEYS, (g + 1) * N_KEYS)
            hs.append(gate * _gelu_tanh(act[es, :]).astype(BF16))
        hmat = jnp.concatenate(hs, axis=0)
        return jnp.dot(vt_ref[c], hmat, preferred_element_type=F32)

    @pl.when(j == 0)
    def _():
        act_ref[0] = up(ufirst_ref[...])

    def pair(k, carry):
        parts = []
        for slot in range(2):
            c = 2 * k + slot
            act_ref[1 - slot] = up(rows_after(c))
            parts.append(down(c, slot))
        acc_ref[...] += parts[0] + parts[1]
        return carry

    lax.fori_loop(0, nch // 2, pair, 0)

    @pl.when(j == pl.num_programs(1) - 1)
    def _():
        y = x2_ref[...] + acc_ref[...].T
        o_ref[...] = _rms(y, gn_ref[...]) if final_norm else y


def _peer_dense(hnt, u, v, nsel, c1, rank2, e2, x2, g_final, final_norm):
    D, T = hnt.shape
    E = u.shape[0]
    tm, te, ch = DENSE_TOKEN_TILE, DENSE_EXPERT_TILE, DENSE_CHUNK
    head_major = pl.BlockSpec((PEER_HEADS, N_KEYS, tm), lambda i, j: (0, 0, i))
    row = pl.BlockSpec((tm, D), lambda i, j: (i, 0))
    u16 = u.astype(BF16)
    vt = v.astype(BF16).reshape(E // ch, ch, D).transpose(0, 2, 1)
    u_window = pl.BlockSpec((pl.Element(te), pl.Element(D)),
                            lambda i, j: (jnp.minimum(j * (te // ch) + 1, (E - te) // ch) * ch, 0))
    return pl.pallas_call(
        functools.partial(_peer_dense_kernel, final_norm=final_norm),
        grid=(T // tm, E // te),
        in_specs=[pl.BlockSpec((D, tm), lambda i, j: (0, i)),
                  pl.BlockSpec((ch, D), lambda i, j: (0, 0)),
                  u_window,
                  pl.BlockSpec((te // ch, D, ch), lambda i, j: (j, 0, 0)),
                  head_major, head_major, head_major, head_major, row,
                  pl.BlockSpec((1, D), lambda i, j: (0, 0))],
        out_specs=row,
        out_shape=jax.ShapeDtypeStruct((T, D), F32),
        scratch_shapes=[pltpu.VMEM((D, tm), F32), pltpu.VMEM((2, ch, tm), F32)],
        compiler_params=_params("parallel", "arbitrary"),
    )(hnt, u16, u16, vt, nsel, c1, rank2, e2, x2, g_final)


def _block_diag(w):
    nb, bi, bo = w.shape
    eye = jnp.eye(nb, dtype=w.dtype)
    return (eye[:, None, :, None] * w[:, :, None, :]).reshape(nb * bi, nb * bo)


def kernel(x, mem, norm_mix, w_in, conv_w, conv_b, gate_a_w, gate_a_b, gate_x_w, gate_x_b, lru_lambda, rel_bias, norm_grp_a, norm_grp_b, w_out, norm_cross, norm_mem, w_q_mem, w_kv_mem, w_o_mem, norm_ffn, w_query, sub_keys, expert_u, expert_v, norm_final):
    B, S, D = x.shape
    depth = w_in.shape[0]
    lru_w = conv_w.shape[-1]
    att_w = (w_in.shape[-1] - 2 * lru_w) // 3
    row = lambda v: v.reshape(1, -1)

    cur = x.reshape(B * S, D)
    mem2d = mem.reshape(B * mem.shape[1], D)
    for l in range(depth):
        zl, q, k, v = _in_proj(cur, row(norm_mix[l]), w_in[l].astype(BF16), 2 * lru_w, att_w)
        w_gates = jnp.concatenate([_block_diag(gate_a_w[l]), _block_diag(gate_x_w[l])],
                                  axis=1).astype(BF16)
        b_gates = jnp.concatenate([gate_a_b[l], gate_x_b[l]]).reshape(1, -1)
        ya = _rg_lru(zl, B, conv_w[l], row(conv_b[l]), w_gates, b_gates,
                     row(lru_lambda[l]), row(norm_grp_a[l]))
        yb = _chunk_attn(q, k, v, B, _attention_bias(rel_bias[l]), row(norm_grp_b[l]))
        kmem, vmem = _mem_kv(mem2d, B, row(norm_mem[l]), w_kv_mem[l].astype(BF16))
        wo = w_out[l].astype(BF16)
        x2, hn3, qp = _mix_cross(cur, ya, yb, wo[:lru_w], wo[lru_w:], row(norm_cross[l]),
                                 w_q_mem[l].astype(BF16), kmem, vmem,
                                 w_o_mem[l].astype(BF16), row(norm_ffn[l]),
                                 w_query[l].astype(BF16), B)
        keys = sub_keys[l].reshape(2 * PEER_HEADS, N_KEYS, -1).astype(BF16)
        nsel, c1, rank2, e2 = _peer_route(qp, keys)
        cur = _peer_dense(hn3, expert_u[l], expert_v[l], nsel, c1, rank2, e2, x2,
                          row(norm_final), l == depth - 1)
    return cur.reshape(B, S, D)
```

```python
import functools

import jax
import jax.numpy as jnp
import numpy as np
from jax import lax
from jax.experimental import pallas as pl
from jax.experimental.pallas import tpu as pltpu

F32 = jnp.float32
BF16 = jnp.bfloat16

EPS = 1e-6
NEG_INF = -1e30

CHUNK = 64
LEFT_CHUNKS = 8
REL_CLIP = 128
CONV_WIDTH = 4
LRU_C = 8.0
ATT_HEADS = 8
MEM_HEADS = 4
PEER_HEADS = 8
N_KEYS = 128
PEER_TOPK = 16

SUBLANES = 8
LANES = 128
VMEM_LIMIT_BYTES = 56 * 1024 * 1024

TOKEN_TILE = 512
ATT_TILE = 4 * CHUNK
ATT_WINDOW = ATT_TILE + LEFT_CHUNKS * CHUNK
ROUTE_TILE = 256
DENSE_TOKEN_TILE = 512
DENSE_EXPERT_TILE = 2048
DENSE_CHUNK = 512


def _params(*semantics, flags=None):
    return pltpu.CompilerParams(dimension_semantics=semantics,
                                vmem_limit_bytes=VMEM_LIMIT_BYTES, flags=flags)


def _rms(xf, g):
    return xf * lax.rsqrt(jnp.mean(xf * xf, axis=-1, keepdims=True) + EPS) * g


def _const_spec(shape):
    nd = len(shape)
    return pl.BlockSpec(shape, lambda *_: (0,) * nd)


def _in_lru_kernel(x_ref, g_ref, wl_ref, wqkv_ref, cw_ref, cb_ref, wg_ref, bg_ref, lam_ref,
                   gn_ref, o_ref, q_ref, k_ref, v_ref, xbuf, hcar):
    ts, w = o_ref.shape
    aw = q_ref.shape[-1]
    hist = SUBLANES

    @pl.when(pl.program_id(1) == 0)
    def _():
        xbuf[0:hist, :] = jnp.zeros((hist, w), F32)
        hcar[...] = jnp.zeros_like(hcar)

    hn = _rms(x_ref[...], g_ref[...]).astype(BF16)
    zl = jnp.dot(hn, wl_ref[...], preferred_element_type=F32)
    qkv = jnp.dot(hn, wqkv_ref[...], preferred_element_type=F32)
    q_ref[...] = qkv[:, :aw].astype(BF16)
    k_ref[...] = qkv[:, aw:2 * aw].astype(BF16)
    v_ref[...] = qkv[:, 2 * aw:].astype(BF16)

    xl = zl[:, :w]
    gate = zl[:, w:]
    xbuf[hist:hist + ts, :] = xl
    xc = cb_ref[...] + cw_ref[CONV_WIDTH - 1:CONV_WIDTH, :] * xl
    for j in range(CONV_WIDTH - 1):
        off = hist - (CONV_WIDTH - 1) + j
        xc = xc + cw_ref[j:j + 1, :] * xbuf[off:off + ts, :]
    xbuf[0:hist, :] = xbuf[ts:ts + hist, :]

    gates = jnp.dot(xc.astype(BF16), wg_ref[...], preferred_element_type=F32) + bg_ref[...]
    r = jax.nn.sigmoid(gates[:, :w])
    ig = jax.nn.sigmoid(gates[:, w:])
    lam = lam_ref[...]
    sp = jnp.log1p(jnp.exp(-jnp.abs(lam))) + jnp.maximum(-lam, 0.0)
    log_a = (-LRU_C) * r * sp
    a = jnp.exp(log_a)
    m2 = jnp.maximum(1.0 - a * a, 0.0)
    mult = jnp.where(m2 > 0.0, m2 * lax.rsqrt(m2), 0.0)
    b = mult * ig * xc

    rows = lax.broadcasted_iota(jnp.int32, (ts, w), 0) % SUBLANES
    d = 1
    while d < SUBLANES:
        keep = rows >= d
        a_prev = jnp.where(keep, pltpu.roll(a, d, 0), 1.0)
        b_prev = jnp.where(keep, pltpu.roll(b, d, 0), 0.0)
        b = a * b_prev + b
        a = a * a_prev
        d *= 2
    carry = hcar[0:1, :]
    groups = []
    for g in range(ts // SUBLANES):
        rs = slice(g * SUBLANES, (g + 1) * SUBLANES)
        hg = b[rs, :] + a[rs, :] * carry
        groups.append(hg)
        carry = hg[SUBLANES - 1:SUBLANES, :]
    h = jnp.concatenate(groups, axis=0)
    hcar[...] = jnp.broadcast_to(carry, hcar.shape)

    y = h * jax.nn.gelu(gate)
    o_ref[...] = _rms(y, gn_ref[...]).astype(o_ref.dtype)


def _in_lru(x2d, batch, g_mix, w_lru, w_qkv, conv_w, conv_b, w_gates, b_gates, lam, g_norm):
    T, D = x2d.shape
    w = conv_w.shape[-1]
    aw = w_qkv.shape[1] // 3
    ts = TOKEN_TILE
    nt = T // batch // ts
    row = lambda n: pl.BlockSpec((ts, n), lambda b, s: (b * nt + s, 0))
    return pl.pallas_call(
        _in_lru_kernel,
        grid=(batch, nt),
        in_specs=[row(D), _const_spec((1, D)), _const_spec(w_lru.shape),
                  _const_spec(w_qkv.shape), _const_spec(conv_w.shape), _const_spec((1, w)),
                  _const_spec(w_gates.shape), _const_spec((1, 2 * w)),
                  _const_spec((1, w)), _const_spec((1, w))],
        out_specs=[row(w), row(aw), row(aw), row(aw)],
        out_shape=[jax.ShapeDtypeStruct((T, w), BF16)]
        + [jax.ShapeDtypeStruct((T, aw), BF16)] * 3,
        scratch_shapes=[pltpu.VMEM((ts + SUBLANES, w), F32), pltpu.VMEM((SUBLANES, w), F32)],
        compiler_params=_params("parallel", "arbitrary"),
    )(x2d, g_mix, w_lru, w_qkv, conv_w, conv_b, w_gates, b_gates, lam, g_norm)


def _chunk_attn_kernel(q_ref, k0_ref, k1_ref, k2_ref, v0_ref, v1_ref, v2_ref, bias_ref,
                       gn_ref, o_ref):
    tq, aw = q_ref.shape
    m = pl.program_id(1)
    dh = aw // ATT_HEADS
    scale = dh ** -0.5
    kw = 3 * tq
    assert LANES == 2 * dh
    col_blk = lax.broadcasted_iota(jnp.int32, (2 * tq, kw), 1) // tq
    valid = (col_blk + m) >= 2
    first = lax.broadcasted_iota(jnp.int32, (1, LANES), 1) < dh
    outs = []
    for pair in range(aw // LANES):
        sl = slice(pair * LANES, (pair + 1) * LANES)
        qp = q_ref[:, sl] * scale
        kp = jnp.concatenate([k0_ref[:, sl], k1_ref[:, sl], k2_ref[:, sl]], axis=0)
        vp = jnp.concatenate([v0_ref[:, sl], v1_ref[:, sl], v2_ref[:, sl]], axis=0)
        zero = jnp.zeros_like(qp)
        q2 = jnp.concatenate([jnp.where(first, qp, zero), jnp.where(first, zero, qp)], axis=0)
        s = lax.dot_general(q2, kp, (((1,), (1,)), ((), ())), preferred_element_type=F32)
        bias = bias_ref[2 * pair:2 * pair + 2].reshape(2 * tq, kw)
        s = jnp.where(valid, s + bias, NEG_INF)
        mx = jnp.max(s, axis=-1, keepdims=True)
        p = jnp.exp(s - mx).astype(BF16)
        ov = jnp.dot(p, jnp.concatenate([vp, jnp.ones_like(vp)], axis=1),
                     preferred_element_type=F32)
        o = ov[:, :LANES] * (1.0 / ov[:, LANES:LANES + 1])
        outs.append(jnp.where(first, o[:tq], o[tq:]))
    y = jnp.concatenate(outs, axis=-1)
    o_ref[...] = _rms(y, gn_ref[...]).astype(o_ref.dtype)


def _attention_bias(rel_bias):
    tq = ATT_TILE
    kw = 3 * tq
    qpos = np.arange(tq)[:, None]
    kpos = np.arange(kw)[None, :] - 2 * tq
    qc = qpos // CHUNK
    kc = np.floor_divide(kpos, CHUNK)
    band = (kc <= qc) & (kc >= qc - LEFT_CHUNKS)
    period = 4 * tq
    j = np.arange(period)
    col_minus_row = np.where(j < kw, j, j - period)
    idx = np.clip(2 * tq - col_minus_row, -REL_CLIP, REL_CLIP) + REL_CLIP
    profile = rel_bias[:, idx].astype(F32)
    heads = profile.shape[0]
    sheared = jnp.tile(profile, (1, tq))[:, :tq * (period - 1)].reshape(heads, tq, period - 1)
    return jnp.where(band[None], sheared[:, :, :kw], NEG_INF)


def _chunk_attn(q, k, v, batch, bias, g_norm):
    T, aw = q.shape
    tq = ATT_TILE
    nt = T // batch // tq
    qspec = pl.BlockSpec((tq, aw), lambda b, m: (b * nt + m, 0))

    def kspec(j):
        return pl.BlockSpec((tq, aw), lambda b, m: (b * nt + jnp.maximum(m - 2 + j, 0), 0))

    return pl.pallas_call(
        _chunk_attn_kernel,
        grid=(batch, nt),
        in_specs=[qspec, kspec(0), kspec(1), kspec(2), kspec(0), kspec(1), kspec(2),
                  _const_spec(bias.shape), _const_spec((1, aw))],
        out_specs=qspec,
        out_shape=jax.ShapeDtypeStruct((T, aw), BF16),
        compiler_params=_params("parallel", "parallel"),
    )(q, k, k, k, v, v, v, bias, g_norm)


def _mem_kv_kernel(m_ref, g_ref, w_ref, k_ref, v_ref):
    mn = _rms(m_ref[...], g_ref[...]).astype(BF16)
    kv = jnp.dot(mn, w_ref[...], preferred_element_type=F32)
    d = k_ref.shape[-1]
    k_ref[...] = kv[:, :d].astype(BF16)
    v_ref[...] = kv[:, d:].astype(BF16)


def _mem_kv(mem2d, batch, g, w_kv):
    R, D = mem2d.shape
    ml = R // batch
    row = pl.BlockSpec((ml, D), lambda b: (b, 0))
    return pl.pallas_call(
        _mem_kv_kernel,
        grid=(batch,),
        in_specs=[row, _const_spec((1, D)), _const_spec(w_kv.shape)],
        out_specs=[row, row],
        out_shape=[jax.ShapeDtypeStruct((R, D), BF16)] * 2,
        compiler_params=_params("parallel"),
    )(mem2d, g, w_kv)


def _mix_cross_kernel(x_ref, ya_ref, yb_ref, woa_ref, wob_ref, gc_ref, wq_ref, km_ref,
                      vm_ref, wo_ref, gf_ref, wqp_ref, x2_ref, hnt_ref, qp_ref):
    x1 = (x_ref[...]
          + jnp.dot(ya_ref[...], woa_ref[...], preferred_element_type=F32)
          + jnp.dot(yb_ref[...], wob_ref[...], preferred_element_type=F32))
    hn2 = _rms(x1, gc_ref[...]).astype(BF16)
    q = jnp.dot(hn2, wq_ref[...], preferred_element_type=F32).astype(BF16)
    d = q.shape[-1]
    dh = d // MEM_HEADS
    scale = dh ** -0.5
    heads = []
    for h in range(MEM_HEADS):
        sl = slice(h * dh, (h + 1) * dh)
        s = lax.dot_general(q[:, sl], km_ref[:, sl], (((1,), (1,)), ((), ())),
                            preferred_element_type=F32) * scale
        mx = jnp.max(s, axis=-1, keepdims=True)
        p = jnp.exp(s - mx)
        l = jnp.sum(p, axis=-1, keepdims=True)
        o = jnp.dot(p.astype(BF16), vm_ref[:, sl], preferred_element_type=F32) * (1.0 / l)
        heads.append(o.astype(BF16))
    o_all = jnp.concatenate(heads, axis=-1)
    x2 = x1 + jnp.dot(o_all, wo_ref[...], preferred_element_type=F32)
    x2_ref[...] = x2
    hn3 = _rms(x2, gf_ref[...])
    hnt_ref[...] = hn3.T.astype(BF16)
    qp_ref[...] = jnp.dot(hn3.astype(BF16), wqp_ref[...],
                          preferred_element_type=F32).astype(BF16)


def _mix_cross(x2d, ya, yb, wo_a, wo_b, g_cross, w_q, kmem, vmem, w_o, g_ffn, w_qp, batch):
    T, D = x2d.shape
    tm = TOKEN_TILE
    per_batch = T // batch // tm
    ml = kmem.shape[0] // batch
    row = lambda w: pl.BlockSpec((tm, w), lambda i: (i, 0))
    mem_spec = pl.BlockSpec((ml, D), lambda i: (i // per_batch, 0))
    qw = w_qp.shape[1]
    return pl.pallas_call(
        _mix_cross_kernel,
        grid=(T // tm,),
        in_specs=[row(D), row(ya.shape[1]), row(yb.shape[1]),
                  _const_spec(wo_a.shape), _const_spec(wo_b.shape), _const_spec((1, D)),
                  _const_spec(w_q.shape), mem_spec, mem_spec, _const_spec(w_o.shape),
                  _const_spec((1, D)), _const_spec(w_qp.shape)],
        out_specs=[row(D), pl.BlockSpec((D, tm), lambda i: (0, i)), row(qw)],
        out_shape=[jax.ShapeDtypeStruct((T, D), F32), jax.ShapeDtypeStruct((D, T), BF16),
                   jax.ShapeDtypeStruct((T, qw), BF16)],
        compiler_params=_params("parallel"),
    )(x2d, ya, yb, wo_a, wo_b, g_cross, w_q, kmem, vmem, w_o, g_ffn, w_qp)


def _vmax(a, b):
    if a is None:
        return b
    if b is None:
        return a
    return jnp.maximum(a, b)


def _vmin(a, b):
    if a is None or b is None:
        return None
    return jnp.minimum(a, b)


def _bitonic_merge_desc(vals):
    n = len(vals)
    vals = list(vals)
    j = n // 2
    while j >= 1:
        for i in range(n):
            l = i ^ j
            if l > i:
                hi, lo = _vmax(vals[i], vals[l]), _vmin(vals[i], vals[l])
                vals[i], vals[l] = hi, lo
        j //= 2
    return vals


def _bitonic_sort_desc(vals):
    n = len(vals)
    vals = list(vals)
    k = 2
    while k <= n:
        j = k // 2
        while j >= 1:
            for i in range(n):
                l = i ^ j
                if l > i:
                    hi, lo = _vmax(vals[i], vals[l]), _vmin(vals[i], vals[l])
                    if (i & k) == 0:
                        vals[i], vals[l] = hi, lo
                    else:
                        vals[i], vals[l] = lo, hi
            j //= 2
        k *= 2
    return vals


def _top16_desc(sc):
    n = sc.shape[-1]
    k = PEER_TOPK
    sc3 = sc.reshape(N_KEYS // SUBLANES, SUBLANES, n)
    vals = _bitonic_sort_desc([sc3[i] for i in range(k)])
    shift = SUBLANES // 2
    while shift >= 1:
        other = [pltpu.roll(v, shift, 0) for v in vals]
        vals = _bitonic_merge_desc([jnp.maximum(vals[i], other[k - 1 - i]) for i in range(k)])
        shift //= 2
    return vals


def _count_prefix16(test, rows):
    t8 = test(rows[7])
    t4 = test(jnp.where(t8, rows[11], rows[3]))
    lo = jnp.where(t4, rows[5], rows[1])
    hi = jnp.where(t4, rows[13], rows[9])
    t2 = test(jnp.where(t8, hi, lo))
    quad = [jnp.where(t2, rows[4 * q + 2], rows[4 * q]) for q in range(4)]
    lo = jnp.where(t4, quad[1], quad[0])
    hi = jnp.where(t4, quad[3], quad[2])
    t1 = test(jnp.where(t8, hi, lo))
    t16 = test(rows[15])
    count = jnp.where(t8, 8.0, 0.0) + jnp.where(t4, 4.0, 0.0)
    count = count + jnp.where(t2, 2.0, 0.0) + jnp.where(t1, 1.0, 0.0)
    return count + jnp.where(t16, 1.0, 0.0)


_CAND = [(a, b) for a in range(PEER_TOPK) for b in range(PEER_TOPK)
         if (a + 1) * (b + 1) <= PEER_TOPK]


def _peer_route_kernel(qp_ref, keys_ref, nsel_ref, c1_ref, rank2_ref, e2_ref):
    tn = qp_ref.shape[0]
    k = PEER_TOPK
    kd = keys_ref.shape[-1]
    sub = lax.broadcasted_iota(jnp.int32, (SUBLANES, tn), 0)

    scores = []
    tops = []
    for hp in range(2 * PEER_HEADS):
        qh = qp_ref[:, hp * kd:(hp + 1) * kd]
        sc = lax.dot_general(keys_ref[hp], qh, (((1,), (1,)), ((), ())),
                             preferred_element_type=F32)
        scores.append(sc)
        tops.append(_top16_desc(sc))

    def pack(p, a):
        out = tops[p][a]
        for h in range(1, PEER_HEADS):
            out = jnp.where(sub == h, tops[2 * h + p][a], out)
        return out

    V1 = [pack(0, a) for a in range(k)]
    V2 = [pack(1, b) for b in range(k)]
    cand = {ab: V1[ab[0]] + V2[ab[1]] for ab in _CAND}
    padded = [cand[ab] for ab in _CAND] + [None] * (64 - len(_CAND))
    tau = _bitonic_sort_desc(padded)[k - 1]
    E1 = [jnp.exp(V1[a] - V1[0]) for a in range(k)]
    E2 = [jnp.exp(V2[b] - V2[0]) for b in range(k)]
    zsum = jnp.zeros((SUBLANES, tn), F32)
    for (a, b) in _CAND:
        zsum = zsum + jnp.where(cand[(a, b)] >= tau, E1[a] * E2[b], 0.0)
    zinv = 1.0 / zsum

    for h in range(PEER_HEADS):
        s1 = scores[2 * h]
        s2 = scores[2 * h + 1]
        slab = 2 * SUBLANES
        row = lambda arr: jnp.broadcast_to(arr[h:h + 1, :], (slab, tn))
        v2_rows = [row(V2[b]) for b in range(k)]
        tau_row = row(tau)
        v1_max, z_inv = row(V1[0]), row(zinv)
        for r in range(N_KEYS // slab):
            rs = slice(r * slab, (r + 1) * slab)
            s1r, s2r = s1[rs, :], s2[rs, :]
            nsel = _count_prefix16(lambda t: (s1r + t) >= tau_row, v2_rows)
            rank2 = _count_prefix16(lambda t: t > s2r, v2_rows)
            nsel_ref[h, rs, :] = nsel
            c1_ref[h, rs, :] = jnp.exp(s1r - v1_max) * z_inv
            rank2_ref[h, rs, :] = rank2.astype(rank2_ref.dtype)
            e2_ref[h, rs, :] = jnp.exp(s2r - v2_rows[0]).astype(e2_ref.dtype)


def _peer_route(qp, keys):
    T = qp.shape[0]
    tn = ROUTE_TILE
    head_major = pl.BlockSpec((PEER_HEADS, N_KEYS, tn), lambda i: (0, 0, i))
    return pl.pallas_call(
        _peer_route_kernel,
        grid=(T // tn,),
        in_specs=[pl.BlockSpec((tn, qp.shape[1]), lambda i: (i, 0)), _const_spec(keys.shape)],
        out_specs=[head_major] * 4,
        out_shape=[jax.ShapeDtypeStruct((PEER_HEADS, N_KEYS, T), F32)] * 2
        + [jax.ShapeDtypeStruct((PEER_HEADS, N_KEYS, T), BF16)] * 2,
        compiler_params=_params("parallel"),
    )(qp, keys)


_GELU_C = float(np.sqrt(2.0 / np.pi))


def _gelu_tanh(x):
    log2e = float(np.log2(np.e))
    k1 = -2.0 * _GELU_C * log2e
    k3 = -2.0 * _GELU_C * 0.044715 * log2e
    e = jnp.exp2(x * (k1 + k3 * (x * x)))
    return x / (1.0 + e)


def _rows_bf16(row, n):
    packed = 2 * SUBLANES
    tile = jnp.broadcast_to(row, (packed, row.shape[-1])).astype(BF16)
    return jnp.tile(tile, (n // packed, 1))


def _peer_dense_kernel(hnt_ref, ufirst_ref, unext_ref, vt_ref, nsel_ref, c1_ref, rank2_ref,
                       e2_ref, x2_ref, gn_ref, o_ref, acc_ref, act_ref, *, final_norm):
    j = pl.program_id(1)
    te = unext_ref.shape[0]

    @pl.when(j == 0)
    def _():
        acc_ref[...] = jnp.zeros_like(acc_ref)

    ch = act_ref.shape[1]
    groups = ch // N_KEYS
    nch = te // ch

    def up(rows):
        return jnp.dot(rows, hnt_ref[...], preferred_element_type=F32)

    def rows_after(c):
        unshifted = (j == pl.num_programs(1) - 1).astype(jnp.int32)
        local = jnp.minimum(c + unshifted, nch - 1)
        return unext_ref[pl.ds(pl.multiple_of(local * ch, ch), ch), :].astype(BF16)

    def down(c, slot):
        act = act_ref[slot]
        hs = []
        for g in range(groups):
            i1 = j * (te // N_KEYS) + c * groups + g
            gate = None
            for h in range(PEER_HEADS):
                n_b = _rows_bf16(nsel_ref[h, pl.ds(i1, 1), :], N_KEYS)
                c_b = _rows_bf16(c1_ref[h, pl.ds(i1, 1), :], N_KEYS)
                term = jnp.where(rank2_ref[h] < n_b, e2_ref[h], jnp.zeros_like(c_b)) * c_b
                gate = term if gate is None else gate + term
            es = slice(g * N_KEYS, (g + 1) * N_KEYS)
            hs.append(gate * _gelu_tanh(act[es, :]).astype(BF16))
        hmat = jnp.concatenate(hs, axis=0)
        return jnp.dot(vt_ref[c], hmat, preferred_element_type=F32)

    @pl.when(j == 0)
    def _():
        act_ref[0] = up(ufirst_ref[...].astype(BF16))

    def pair(k, carry):
        parts = []
        for slot in range(2):
            c = 2 * k + slot
            act_ref[1 - slot] = up(rows_after(c))
            parts.append(down(c, slot))
        acc_ref[...] += parts[0] + parts[1]
        return carry

    lax.fori_loop(0, nch // 2, pair, 0)

    @pl.when(j == pl.num_programs(1) - 1)
    def _():
        y = x2_ref[...] + acc_ref[...].T
        o_ref[...] = _rms(y, gn_ref[...]) if final_norm else y


def _peer_dense(hnt, u, v, nsel, c1, rank2, e2, x2, g_final, final_norm):
    D, T = hnt.shape
    E = u.shape[0]
    tm, te, ch = DENSE_TOKEN_TILE, DENSE_EXPERT_TILE, DENSE_CHUNK
    head_major = pl.BlockSpec((PEER_HEADS, N_KEYS, tm), lambda i, j: (0, 0, i))
    row = pl.BlockSpec((tm, D), lambda i, j: (i, 0))
    u16 = u.astype(BF16)
    vt = v.astype(BF16).reshape(E // ch, ch, D).transpose(0, 2, 1)
    u_window = pl.BlockSpec((pl.Element(te), pl.Element(D)),
                            lambda i, j: (jnp.minimum(j * (te // ch) + 1, (E - te) // ch) * ch, 0))
    return pl.pallas_call(
        functools.partial(_peer_dense_kernel, final_norm=final_norm),
        grid=(T // tm, E // te),
        in_specs=[pl.BlockSpec((D, tm), lambda i, j: (0, i)),
                  pl.BlockSpec((ch, D), lambda i, j: (0, 0)),
                  u_window,
                  pl.BlockSpec((te // ch, D, ch), lambda i, j: (j, 0, 0)),
                  head_major, head_major, head_major, head_major, row,
                  pl.BlockSpec((1, D), lambda i, j: (0, 0))],
        out_specs=row,
        out_shape=jax.ShapeDtypeStruct((T, D), F32),
        scratch_shapes=[pltpu.VMEM((D, tm), F32), pltpu.VMEM((2, ch, tm), F32)],
        compiler_params=_params("parallel", "arbitrary"),
    )(hnt, u, u, vt, nsel, c1, rank2, e2, x2, g_final)


def _block_diag(w):
    nb, bi, bo = w.shape
    eye = jnp.eye(nb, dtype=w.dtype)
    return (eye[:, None, :, None] * w[:, :, None, :]).reshape(nb * bi, nb * bo)


def kernel(x, mem, norm_mix, w_in, conv_w, conv_b, gate_a_w, gate_a_b, gate_x_w, gate_x_b, lru_lambda, rel_bias, norm_grp_a, norm_grp_b, w_out, norm_cross, norm_mem, w_q_mem, w_kv_mem, w_o_mem, norm_ffn, w_query, sub_keys, expert_u, expert_v, norm_final):
    B, S, D = x.shape
    depth = w_in.shape[0]
    lru_w = conv_w.shape[-1]
    att_w = (w_in.shape[-1] - 2 * lru_w) // 3
    row = lambda v: v.reshape(1, -1)

    cur = x.reshape(B * S, D)
    mem2d = mem.reshape(B * mem.shape[1], D)
    for l in range(depth):
        w_gates = jnp.concatenate([_block_diag(gate_a_w[l]), _block_diag(gate_x_w[l])],
                                  axis=1).astype(BF16)
        b_gates = jnp.concatenate([gate_a_b[l], gate_x_b[l]]).reshape(1, -1)
        w_in16 = w_in[l].astype(BF16)
        ya, q, k, v = _in_lru(cur, B, row(norm_mix[l]), w_in16[:, :2 * lru_w],
                              w_in16[:, 2 * lru_w:], conv_w[l], row(conv_b[l]), w_gates,
                              b_gates, row(lru_lambda[l]), row(norm_grp_a[l]))
        yb = _chunk_attn(q, k, v, B, _attention_bias(rel_bias[l]), row(norm_grp_b[l]))
        kmem, vmem = _mem_kv(mem2d, B, row(norm_mem[l]), w_kv_mem[l].astype(BF16))
        wo = w_out[l].astype(BF16)
        x2, hn3, qp = _mix_cross(cur, ya, yb, wo[:lru_w], wo[lru_w:], row(norm_cross[l]),
                                 w_q_mem[l].astype(BF16), kmem, vmem,
                                 w_o_mem[l].astype(BF16), row(norm_ffn[l]),
                                 w_query[l].astype(BF16), B)
        keys = sub_keys[l].reshape(2 * PEER_HEADS, N_KEYS, -1).astype(BF16)
        nsel, c1, rank2, e2 = _peer_route(qp, keys)
        cur = _peer_dense(hn3, expert_u[l], expert_v[l], nsel, c1, rank2, e2, x2,
                          row(norm_final), l == depth - 1)
    return cur.reshape(B, S, D)
```

```python
import functools

import jax
import jax.numpy as jnp
import numpy as np
from jax import lax
from jax.experimental import pallas as pl
from jax.experimental.pallas import tpu as pltpu

F32 = jnp.float32
BF16 = jnp.bfloat16

EPS = 1e-6
NEG_INF = -1e30

CHUNK = 64
LEFT_CHUNKS = 8
REL_CLIP = 128
CONV_WIDTH = 4
LRU_C = 8.0
ATT_HEADS = 8
MEM_HEADS = 4
PEER_HEADS = 8
N_KEYS = 128
PEER_TOPK = 16

SUBLANES = 8
LANES = 128
VMEM_LIMIT_BYTES = 56 * 1024 * 1024

TOKEN_TILE = 512
MIX_TILE = 1024
ATT_TILE = 4 * CHUNK
ATT_WINDOW = ATT_TILE + LEFT_CHUNKS * CHUNK
ROUTE_TILE = 256
DENSE_TOKEN_TILE = 512
DENSE_EXPERT_TILE = 2048
DENSE_CHUNK = 512


def _params(*semantics, flags=None):
    return pltpu.CompilerParams(dimension_semantics=semantics,
                                vmem_limit_bytes=VMEM_LIMIT_BYTES, flags=flags)


def _rms(xf, g):
    return xf * lax.rsqrt(jnp.mean(xf * xf, axis=-1, keepdims=True) + EPS) * g


def _const_spec(shape):
    nd = len(shape)
    return pl.BlockSpec(shape, lambda *_: (0,) * nd, pipeline_mode=pl.Buffered(1))


def _in_lru_kernel(x_ref, g_ref, wl_ref, wqkv_ref, cw_ref, cb_ref, wg_ref, bg_ref, lam_ref,
                   gn_ref, o_ref, q_ref, k_ref, v_ref, xbuf, hcar):
    ts, w = o_ref.shape
    aw = q_ref.shape[-1]
    hist = SUBLANES

    @pl.when(pl.program_id(1) == 0)
    def _():
        xbuf[0:hist, :] = jnp.zeros((hist, w), F32)
        hcar[...] = jnp.zeros_like(hcar)

    hn = _rms(x_ref[...], g_ref[...]).astype(BF16)
    zl = jnp.dot(hn, wl_ref[...], preferred_element_type=F32)
    qkv = jnp.dot(hn, wqkv_ref[...], preferred_element_type=F32)
    q_ref[...] = qkv[:, :aw].astype(BF16)
    k_ref[...] = qkv[:, aw:2 * aw].astype(BF16)
    v_ref[...] = qkv[:, 2 * aw:].astype(BF16)

    xl = zl[:, :w]
    gate = zl[:, w:]
    xbuf[hist:hist + ts, :] = xl
    xc = cb_ref[...] + cw_ref[CONV_WIDTH - 1:CONV_WIDTH, :] * xl
    for j in range(CONV_WIDTH - 1):
        off = hist - (CONV_WIDTH - 1) + j
        xc = xc + cw_ref[j:j + 1, :] * xbuf[off:off + ts, :]
    xbuf[0:hist, :] = xbuf[ts:ts + hist, :]

    gates = jnp.dot(xc.astype(BF16), wg_ref[...], preferred_element_type=F32) + bg_ref[...]
    r = jax.nn.sigmoid(gates[:, :w])
    ig = jax.nn.sigmoid(gates[:, w:])
    lam = lam_ref[...]
    sp = jnp.log1p(jnp.exp(-jnp.abs(lam))) + jnp.maximum(-lam, 0.0)
    log_a = (-LRU_C) * r * sp
    a = jnp.exp(log_a)
    m2 = jnp.maximum(1.0 - a * a, 0.0)
    mult = jnp.where(m2 > 0.0, m2 * lax.rsqrt(m2), 0.0)
    b = mult * ig * xc

    rows = lax.broadcasted_iota(jnp.int32, (ts, w), 0) % SUBLANES
    d = 1
    while d < SUBLANES:
        keep = rows >= d
        a_prev = jnp.where(keep, pltpu.roll(a, d, 0), 1.0)
        b_prev = jnp.where(keep, pltpu.roll(b, d, 0), 0.0)
        b = a * b_prev + b
        a = a * a_prev
        d *= 2
    carry = hcar[0:1, :]
    groups = []
    for g in range(ts // SUBLANES):
        rs = slice(g * SUBLANES, (g + 1) * SUBLANES)
        hg = b[rs, :] + a[rs, :] * carry
        groups.append(hg)
        carry = hg[SUBLANES - 1:SUBLANES, :]
    h = jnp.concatenate(groups, axis=0)
    hcar[...] = jnp.broadcast_to(carry, hcar.shape)

    y = h * jax.nn.gelu(gate)
    o_ref[...] = _rms(y, gn_ref[...]).astype(o_ref.dtype)


def _in_lru(x2d, batch, g_mix, w_lru, w_qkv, conv_w, conv_b, w_gates, b_gates, lam, g_norm):
    T, D = x2d.shape
    w = conv_w.shape[-1]
    aw = w_qkv.shape[1] // 3
    ts = TOKEN_TILE
    nt = T // batch // ts
    row = lambda n: pl.BlockSpec((ts, n), lambda b, s: (b * nt + s, 0))
    return pl.pallas_call(
        _in_lru_kernel,
        grid=(batch, nt),
        in_specs=[row(D), _const_spec((1, D)), _const_spec(w_lru.shape),
                  _const_spec(w_qkv.shape), _const_spec(conv_w.shape), _const_spec((1, w)),
                  _const_spec(w_gates.shape), _const_spec((1, 2 * w)),
                  _const_spec((1, w)), _const_spec((1, w))],
        out_specs=[row(w), row(aw), row(aw), row(aw)],
        out_shape=[jax.ShapeDtypeStruct((T, w), BF16)]
        + [jax.ShapeDtypeStruct((T, aw), BF16)] * 3,
        scratch_shapes=[pltpu.VMEM((ts + SUBLANES, w), F32), pltpu.VMEM((SUBLANES, w), F32)],
        compiler_params=_params("parallel", "arbitrary"),
    )(x2d, g_mix, w_lru, w_qkv, conv_w, conv_b, w_gates, b_gates, lam, g_norm)


def _chunk_attn_kernel(q_ref, k0_ref, k1_ref, k2_ref, k3_ref, v0_ref, v1_ref, v2_ref, v3_ref,
                       bias_ref, gn_ref, o_ref):
    aw = q_ref.shape[1]
    tq = k0_ref.shape[0]
    dh = aw // ATT_HEADS
    scale = dh ** -0.5
    kw = 3 * tq
    assert LANES == 2 * dh
    k_refs = (k0_ref, k1_ref, k2_ref, k3_ref)
    v_refs = (v0_ref, v1_ref, v2_ref, v3_ref)
    col_blk = lax.broadcasted_iota(jnp.int32, (2 * tq, kw), 1) // tq
    first = lax.broadcasted_iota(jnp.int32, (1, LANES), 1) < dh
    for t in range(2):
        m = 2 * pl.program_id(1) + t
        valid = (col_blk + m) >= 2
        rows = slice(t * tq, (t + 1) * tq)
        outs = []
        for pair in range(aw // LANES):
            sl = slice(pair * LANES, (pair + 1) * LANES)
            qp = q_ref[rows, sl] * scale
            kp = jnp.concatenate([r[:, sl] for r in k_refs[t:t + 3]], axis=0)
            vp = jnp.concatenate([r[:, sl] for r in v_refs[t:t + 3]], axis=0)
            zero = jnp.zeros_like(qp)
            q2 = jnp.concatenate([jnp.where(first, qp, zero), jnp.where(first, zero, qp)],
                                 axis=0)
            s = lax.dot_general(q2, kp, (((1,), (1,)), ((), ())), preferred_element_type=F32)
            bias = bias_ref[2 * pair:2 * pair + 2].reshape(2 * tq, kw)
            s = jnp.where(valid, s + bias, NEG_INF)
            mx = jnp.max(s, axis=-1, keepdims=True)
            p = jnp.exp(s - mx).astype(BF16)
            ov = jnp.dot(p, jnp.concatenate([vp, jnp.ones_like(vp)], axis=1),
                         preferred_element_type=F32)
            o = ov[:, :LANES] * (1.0 / ov[:, LANES:LANES + 1])
            outs.append(jnp.where(first, o[:tq], o[tq:]))
        y = jnp.concatenate(outs, axis=-1)
        o_ref[rows, :] = _rms(y, gn_ref[...]).astype(o_ref.dtype)


def _attention_bias(rel_bias):
    tq = ATT_TILE
    kw = 3 * tq
    qpos = np.arange(tq)[:, None]
    kpos = np.arange(kw)[None, :] - 2 * tq
    qc = qpos // CHUNK
    kc = np.floor_divide(kpos, CHUNK)
    band = (kc <= qc) & (kc >= qc - LEFT_CHUNKS)
    period = 4 * tq
    j = np.arange(period)
    col_minus_row = np.where(j < kw, j, j - period)
    idx = np.clip(2 * tq - col_minus_row, -REL_CLIP, REL_CLIP) + REL_CLIP
    profile = rel_bias[:, idx].astype(F32)
    heads = profile.shape[0]
    sheared = jnp.tile(profile, (1, tq))[:, :tq * (period - 1)].reshape(heads, tq, period - 1)
    return jnp.where(band[None], sheared[:, :, :kw], NEG_INF)


def _chunk_attn(q, k, v, batch, bias, g_norm):
    T, aw = q.shape
    tq = ATT_TILE
    nt = T // batch // tq
    steps = nt // 2
    qspec = pl.BlockSpec((2 * tq, aw), lambda b, m: (b * steps + m, 0))

    def kspec(j):
        return pl.BlockSpec((tq, aw),
                            lambda b, m: (b * nt + jnp.maximum(2 * m - 2 + j, 0), 0))

    kv_specs = [kspec(j) for j in range(4)]
    return pl.pallas_call(
        _chunk_attn_kernel,
        grid=(batch, steps),
        in_specs=[qspec] + kv_specs + kv_specs
        + [_const_spec(bias.shape), _const_spec((1, aw))],
        out_specs=qspec,
        out_shape=jax.ShapeDtypeStruct((T, aw), BF16),
        compiler_params=_params("parallel", "parallel"),
    )(q, k, k, k, k, v, v, v, v, bias, g_norm)


def _mem_kv_kernel(m_ref, g_ref, w_ref, k_ref, v_ref):
    mn = _rms(m_ref[...], g_ref[...]).astype(BF16)
    kv = jnp.dot(mn, w_ref[...], preferred_element_type=F32)
    d = k_ref.shape[-1]
    k_ref[...] = kv[:, :d].astype(BF16)
    v_ref[...] = kv[:, d:].astype(BF16)


def _mem_kv(mem2d, batch, g, w_kv):
    R, D = mem2d.shape
    ml = R // batch
    row = pl.BlockSpec((ml, D), lambda b: (b, 0))
    return pl.pallas_call(
        _mem_kv_kernel,
        grid=(batch,),
        in_specs=[row, _const_spec((1, D)), _const_spec(w_kv.shape)],
        out_specs=[row, row],
        out_shape=[jax.ShapeDtypeStruct((R, D), BF16)] * 2,
        compiler_params=_params("parallel"),
    )(mem2d, g, w_kv)


def _mix_cross_kernel(x_ref, ya_ref, yb_ref, woa_ref, wob_ref, gc_ref, wq_ref, km_ref,
                      vm_ref, wo_ref, gf_ref, wqp_ref, x2_ref, hnt_ref, qp_ref):
    x1 = (x_ref[...]
          + jnp.dot(ya_ref[...], woa_ref[...], preferred_element_type=F32)
          + jnp.dot(yb_ref[...], wob_ref[...], preferred_element_type=F32))
    hn2 = _rms(x1, gc_ref[...]).astype(BF16)
    q = jnp.dot(hn2, wq_ref[...], preferred_element_type=F32).astype(BF16)
    d = q.shape[-1]
    dh = d // MEM_HEADS
    scale = dh ** -0.5
    heads = []
    for h in range(MEM_HEADS):
        sl = slice(h * dh, (h + 1) * dh)
        s = lax.dot_general(q[:, sl], km_ref[:, sl], (((1,), (1,)), ((), ())),
                            preferred_element_type=F32) * scale
        mx = jnp.max(s, axis=-1, keepdims=True)
        p = jnp.exp(s - mx)
        l = jnp.sum(p, axis=-1, keepdims=True)
        o = jnp.dot(p.astype(BF16), vm_ref[:, sl], preferred_element_type=F32) * (1.0 / l)
        heads.append(o.astype(BF16))
    o_all = jnp.concatenate(heads, axis=-1)
    x2 = x1 + jnp.dot(o_all, wo_ref[...], preferred_element_type=F32)
    x2_ref[...] = x2
    hn3 = _rms(x2, gf_ref[...])
    hnt_ref[...] = hn3.T.astype(BF16)
    qp_ref[...] = jnp.dot(hn3.astype(BF16), wqp_ref[...],
                          preferred_element_type=F32).astype(BF16)


def _mix_cross(x2d, ya, yb, wo_a, wo_b, g_cross, w_q, kmem, vmem, w_o, g_ffn, w_qp, batch):
    T, D = x2d.shape
    tm = MIX_TILE
    per_batch = T // batch // tm
    ml = kmem.shape[0] // batch
    row = lambda w: pl.BlockSpec((tm, w), lambda i: (i, 0))
    mem_spec = pl.BlockSpec((ml, D), lambda i: (i // per_batch, 0))
    qw = w_qp.shape[1]
    return pl.pallas_call(
        _mix_cross_kernel,
        grid=(T // tm,),
        in_specs=[row(D), row(ya.shape[1]), row(yb.shape[1]),
                  _const_spec(wo_a.shape), _const_spec(wo_b.shape), _const_spec((1, D)),
                  _const_spec(w_q.shape), mem_spec, mem_spec, _const_spec(w_o.shape),
                  _const_spec((1, D)), _const_spec(w_qp.shape)],
        out_specs=[row(D), pl.BlockSpec((D, tm), lambda i: (0, i)), row(qw)],
        out_shape=[jax.ShapeDtypeStruct((T, D), F32), jax.ShapeDtypeStruct((D, T), BF16),
                   jax.ShapeDtypeStruct((T, qw), BF16)],
        compiler_params=_params("parallel"),
    )(x2d, ya, yb, wo_a, wo_b, g_cross, w_q, kmem, vmem, w_o, g_ffn, w_qp)


def _vmax(a, b):
    if a is None:
        return b
    if b is None:
        return a
    return jnp.maximum(a, b)


def _vmin(a, b):
    if a is None or b is None:
        return None
    return jnp.minimum(a, b)


def _bitonic_merge_desc(vals):
    n = len(vals)
    vals = list(vals)
    j = n // 2
    while j >= 1:
        for i in range(n):
            l = i ^ j
            if l > i:
                hi, lo = _vmax(vals[i], vals[l]), _vmin(vals[i], vals[l])
                vals[i], vals[l] = hi, lo
        j //= 2
    return vals


def _bitonic_sort_desc(vals):
    n = len(vals)
    vals = list(vals)
    k = 2
    while k <= n:
        j = k // 2
        while j >= 1:
            for i in range(n):
                l = i ^ j
                if l > i:
                    hi, lo = _vmax(vals[i], vals[l]), _vmin(vals[i], vals[l])
                    if (i & k) == 0:
                        vals[i], vals[l] = hi, lo
                    else:
                        vals[i], vals[l] = lo, hi
            j //= 2
        k *= 2
    return vals


def _top16_desc(sc):
    n = sc.shape[-1]
    k = PEER_TOPK
    sc3 = sc.reshape(N_KEYS // SUBLANES, SUBLANES, n)
    vals = _bitonic_sort_desc([sc3[i] for i in range(k)])
    shift = SUBLANES // 2
    while shift >= 1:
        other = [pltpu.roll(v, shift, 0) for v in vals]
        vals = _bitonic_merge_desc([jnp.maximum(vals[i], other[k - 1 - i]) for i in range(k)])
        shift //= 2
    return vals


def _count_prefix16(test, rows):
    t8 = test(rows[7])
    t4 = test(jnp.where(t8, rows[11], rows[3]))
    lo = jnp.where(t4, rows[5], rows[1])
    hi = jnp.where(t4, rows[13], rows[9])
    t2 = test(jnp.where(t8, hi, lo))
    quad = [jnp.where(t2, rows[4 * q + 2], rows[4 * q]) for q in range(4)]
    lo = jnp.where(t4, quad[1], quad[0])
    hi = jnp.where(t4, quad[3], quad[2])
    t1 = test(jnp.where(t8, hi, lo))
    t16 = test(rows[15])
    count = jnp.where(t8, 8.0, 0.0) + jnp.where(t4, 4.0, 0.0)
    count = count + jnp.where(t2, 2.0, 0.0) + jnp.where(t1, 1.0, 0.0)
    return count + jnp.where(t16, 1.0, 0.0)


_CAND = [(a, b) for a in range(PEER_TOPK) for b in range(PEER_TOPK)
         if (a + 1) * (b + 1) <= PEER_TOPK]


def _peer_route_kernel(qp_ref, keys_ref, nsel_ref, c1_ref, rank2_ref, e2_ref):
    tn = qp_ref.shape[0]
    k = PEER_TOPK
    kd = keys_ref.shape[-1]
    sub = lax.broadcasted_iota(jnp.int32, (SUBLANES, tn), 0)

    scores = []
    tops = []
    for hp in range(2 * PEER_HEADS):
        qh = qp_ref[:, hp * kd:(hp + 1) * kd]
        sc = lax.dot_general(keys_ref[hp], qh, (((1,), (1,)), ((), ())),
                             preferred_element_type=F32)
        scores.append(sc)
        tops.append(_top16_desc(sc))

    def pack(p, a):
        out = tops[p][a]
        for h in range(1, PEER_HEADS):
            out = jnp.where(sub == h, tops[2 * h + p][a], out)
        return out

    V1 = [pack(0, a) for a in range(k)]
    V2 = [pack(1, b) for b in range(k)]
    cand = {ab: V1[ab[0]] + V2[ab[1]] for ab in _CAND}
    padded = [cand[ab] for ab in _CAND] + [None] * (64 - len(_CAND))
    tau = _bitonic_sort_desc(padded)[k - 1]
    E1 = [jnp.exp(V1[a] - V1[0]) for a in range(k)]
    E2 = [jnp.exp(V2[b] - V2[0]) for b in range(k)]
    zsum = jnp.zeros((SUBLANES, tn), F32)
    for (a, b) in _CAND:
        zsum = zsum + jnp.where(cand[(a, b)] >= tau, E1[a] * E2[b], 0.0)
    zinv = 1.0 / zsum

    for h in range(PEER_HEADS):
        s1 = scores[2 * h]
        s2 = scores[2 * h + 1]
        slab = 2 * SUBLANES
        row = lambda arr: jnp.broadcast_to(arr[h:h + 1, :], (slab, tn))
        v2_rows = [row(V2[b]) for b in range(k)]
        tau_row = row(tau)
        v1_max, z_inv = row(V1[0]), row(zinv)
        for r in range(N_KEYS // slab):
            rs = slice(r * slab, (r + 1) * slab)
            s1r, s2r = s1[rs, :], s2[rs, :]
            nsel = _count_prefix16(lambda t: (s1r + t) >= tau_row, v2_rows)
            rank2 = _count_prefix16(lambda t: t > s2r, v2_rows)
            nsel_ref[h, rs, :] = nsel
            c1_ref[h, rs, :] = jnp.exp(s1r - v1_max) * z_inv
            rank2_ref[h, rs, :] = rank2.astype(rank2_ref.dtype)
            e2_ref[h, rs, :] = jnp.exp(s2r - v2_rows[0]).astype(e2_ref.dtype)


def _peer_route(qp, keys):
    T = qp.shape[0]
    tn = ROUTE_TILE
    head_major = pl.BlockSpec((PEER_HEADS, N_KEYS, tn), lambda i: (0, 0, i))
    return pl.pallas_call(
        _peer_route_kernel,
        grid=(T // tn,),
        in_specs=[pl.BlockSpec((tn, qp.shape[1]), lambda i: (i, 0)), _const_spec(keys.shape)],
        out_specs=[head_major] * 4,
        out_shape=[jax.ShapeDtypeStruct((PEER_HEADS, N_KEYS, T), F32)] * 2
        + [jax.ShapeDtypeStruct((PEER_HEADS, N_KEYS, T), BF16)] * 2,
        compiler_params=_params("parallel"),
    )(qp, keys)


_GELU_C = float(np.sqrt(2.0 / np.pi))


def _gelu_tanh(x):
    log2e = float(np.log2(np.e))
    k1 = -2.0 * _GELU_C * log2e
    k3 = -2.0 * _GELU_C * 0.044715 * log2e
    e = jnp.exp2(x * (k1 + k3 * (x * x)))
    return x / (1.0 + e)


def _rows_bf16(row, n):
    packed = 2 * SUBLANES
    tile = jnp.broadcast_to(row, (packed, row.shape[-1])).astype(BF16)
    return jnp.tile(tile, (n // packed, 1))


def _peer_dense_kernel(hnt_ref, ufirst_ref, unext_ref, vt_ref, nsel_ref, c1_ref, rank2_ref,
                       e2_ref, x2_ref, gn_ref, o_ref, acc_ref, act_ref, *, final_norm):
    j = pl.program_id(1)
    te = unext_ref.shape[0]

    @pl.when(j == 0)
    def _():
        acc_ref[...] = jnp.zeros_like(acc_ref)

    ch = act_ref.shape[1]
    groups = ch // N_KEYS
    nch = te // ch

    def up(rows):
        return jnp.dot(rows, hnt_ref[...], preferred_element_type=F32)

    def rows_after(c):
        unshifted = (j == pl.num_programs(1) - 1).astype(jnp.int32)
        local = jnp.minimum(c + unshifted, nch - 1)
        return unext_ref[pl.ds(pl.multiple_of(local * ch, ch), ch), :].astype(BF16)

    def down(c, slot):
        act = act_ref[slot]
        hs = []
        for g in range(groups):
            i1 = j * (te // N_KEYS) + c * groups + g
            gate = None
            for h in range(PEER_HEADS):
                n_b = _rows_bf16(nsel_ref[h, pl.ds(i1, 1), :], N_KEYS)
                c_b = _rows_bf16(c1_ref[h, pl.ds(i1, 1), :], N_KEYS)
                term = jnp.where(rank2_ref[h] < n_b, e2_ref[h], jnp.zeros_like(c_b)) * c_b
                gate = term if gate is None else gate + term
            es = slice(g * N_KEYS, (g + 1) * N_KEYS)
            hs.append(gate * _gelu_tanh(act[es, :]).astype(BF16))
        hmat = jnp.concatenate(hs, axis=0)
        return jnp.dot(vt_ref[c], hmat, preferred_element_type=F32)

    @pl.when(j == 0)
    def _():
        act_ref[0] = up(ufirst_ref[...].astype(BF16))

    def pair(k, carry):
        parts = []
        for slot in range(2):
            c = 2 * k + slot
            act_ref[1 - slot] = up(rows_after(c))
            parts.append(down(c, slot))
        acc_ref[...] += parts[0] + parts[1]
        return carry

    lax.fori_loop(0, nch // 2, pair, 0)

    @pl.when(j == pl.num_programs(1) - 1)
    def _():
        y = x2_ref[...] + acc_ref[...].T
        o_ref[...] = _rms(y, gn_ref[...]) if final_norm else y


def _peer_dense(hnt, u, v, nsel, c1, rank2, e2, x2, g_final, final_norm):
    D, T = hnt.shape
    E = u.shape[0]
    tm, te, ch = DENSE_TOKEN_TILE, DENSE_EXPERT_TILE, DENSE_CHUNK
    head_major = pl.BlockSpec((PEER_HEADS, N_KEYS, tm), lambda i, j: (0, 0, i))
    row = pl.BlockSpec((tm, D), lambda i, j: (i, 0))
    vt = v.astype(BF16).reshape(E // ch, ch, D).transpose(0, 2, 1)
    u_window = pl.BlockSpec((pl.Element(te), pl.Element(D)),
                            lambda i, j: (jnp.minimum(j * (te // ch) + 1, (E - te) // ch) * ch, 0))
    return pl.pallas_call(
        functools.partial(_peer_dense_kernel, final_norm=final_norm),
        grid=(T // tm, E // te),
        in_specs=[pl.BlockSpec((D, tm), lambda i, j: (0, i)),
                  pl.BlockSpec((ch, D), lambda i, j: (0, 0)),
                  u_window,
                  pl.BlockSpec((te // ch, D, ch), lambda i, j: (j, 0, 0)),
                  head_major, head_major, head_major, head_major, row,
                  pl.BlockSpec((1, D), lambda i, j: (0, 0))],
        out_specs=row,
        out_shape=jax.ShapeDtypeStruct((T, D), F32),
        scratch_shapes=[pltpu.VMEM((D, tm), F32), pltpu.VMEM((2, ch, tm), F32)],
        compiler_params=_params("parallel", "arbitrary"),
    )(hnt, u, u, vt, nsel, c1, rank2, e2, x2, g_final)


def _block_diag(w):
    nb, bi, bo = w.shape
    eye = jnp.eye(nb, dtype=w.dtype)
    return (eye[:, None, :, None] * w[:, :, None, :]).reshape(nb * bi, nb * bo)


def kernel(x, mem, norm_mix, w_in, conv_w, conv_b, gate_a_w, gate_a_b, gate_x_w, gate_x_b, lru_lambda, rel_bias, norm_grp_a, norm_grp_b, w_out, norm_cross, norm_mem, w_q_mem, w_kv_mem, w_o_mem, norm_ffn, w_query, sub_keys, expert_u, expert_v, norm_final):
    B, S, D = x.shape
    depth = w_in.shape[0]
    lru_w = conv_w.shape[-1]
    att_w = (w_in.shape[-1] - 2 * lru_w) // 3
    row = lambda v: v.reshape(1, -1)

    cur = x.reshape(B * S, D)
    mem2d = mem.reshape(B * mem.shape[1], D)
    for l in range(depth):
        w_gates = jnp.concatenate([_block_diag(gate_a_w[l]), _block_diag(gate_x_w[l])],
                                  axis=1).astype(BF16)
        b_gates = jnp.concatenate([gate_a_b[l], gate_x_b[l]]).reshape(1, -1)
        w_in16 = w_in[l].astype(BF16)
        ya, q, k, v = _in_lru(cur, B, row(norm_mix[l]), w_in16[:, :2 * lru_w],
                              w_in16[:, 2 * lru_w:], conv_w[l], row(conv_b[l]), w_gates,
                              b_gates, row(lru_lambda[l]), row(norm_grp_a[l]))
        yb = _chunk_attn(q, k, v, B, _attention_bias(rel_bias[l]), row(norm_grp_b[l]))
        kmem, vmem = _mem_kv(mem2d, B, row(norm_mem[l]), w_kv_mem[l].astype(BF16))
        wo = w_out[l].astype(BF16)
        x2, hn3, qp = _mix_cross(cur, ya, yb, wo[:lru_w], wo[lru_w:], row(norm_cross[l]),
                                 w_q_mem[l].astype(BF16), kmem, vmem,
                                 w_o_mem[l].astype(BF16), row(norm_ffn[l]),
                                 w_query[l].astype(BF16), B)
        keys = sub_keys[l].reshape(2 * PEER_HEADS, N_KEYS, -1).astype(BF16)
        nsel, c1, rank2, e2 = _peer_route(qp, keys)
        cur = _peer_dense(hn3, expert_u[l], expert_v[l], nsel, c1, rank2, e2, x2,
                          row(norm_final), l == depth - 1)
    return cur.reshape(B, S, D)
```

```python
import functools

import jax
import jax.numpy as jnp
import numpy as np
from jax import lax
from jax.experimental import pallas as pl
from jax.experimental.pallas import tpu as pltpu

F32 = jnp.float32
BF16 = jnp.bfloat16

EPS = 1e-6
NEG_INF = -1e30

CHUNK = 64
LEFT_CHUNKS = 8
REL_CLIP = 128
CONV_WIDTH = 4
LRU_C = 8.0
ATT_HEADS = 8
MEM_HEADS = 4
PEER_HEADS = 8
N_KEYS = 128
PEER_TOPK = 16

SUBLANES = 8
LANES = 128
VMEM_LIMIT_BYTES = 56 * 1024 * 1024

TOKEN_TILE = 1024
MIX_TILE = 1024
ATT_TILE = 4 * CHUNK
ATT_STEP_TILES = 4
ROUTE_TILE = 256
DENSE_TOKEN_TILE = 512
DENSE_EXPERT_TILE = 2048
DENSE_CHUNK = 512


def _params(*semantics, flags=None):
    return pltpu.CompilerParams(dimension_semantics=semantics,
                                vmem_limit_bytes=VMEM_LIMIT_BYTES, flags=flags)


def _rms(xf, g):
    return xf * lax.rsqrt(jnp.mean(xf * xf, axis=-1, keepdims=True) + EPS) * g


def _const_spec(shape):
    nd = len(shape)
    return pl.BlockSpec(shape, lambda *_: (0,) * nd, pipeline_mode=pl.Buffered(1))


def _in_lru_kernel(x_ref, g_ref, wl_ref, wqkv_ref, cw_ref, cb_ref, wg_ref, bg_ref, lam_ref,
                   gn_ref, o_ref, q_ref, k_ref, v_ref, xbuf, hcar):
    ts, w = o_ref.shape
    aw = q_ref.shape[-1]
    hist = SUBLANES

    @pl.when(pl.program_id(1) == 0)
    def _():
        xbuf[0:hist, :] = jnp.zeros((hist, w), F32)
        hcar[...] = jnp.zeros_like(hcar)

    hn = _rms(x_ref[...], g_ref[...]).astype(BF16)
    zl = jnp.dot(hn, wl_ref[...], preferred_element_type=F32)
    qkv = jnp.dot(hn, wqkv_ref[...], preferred_element_type=F32)
    q_ref[...] = qkv[:, :aw].astype(BF16)
    k_ref[...] = qkv[:, aw:2 * aw].astype(BF16)
    v_ref[...] = qkv[:, 2 * aw:].astype(BF16)

    xl = zl[:, :w]
    gate = zl[:, w:]
    xbuf[hist:hist + ts, :] = xl
    xc = cb_ref[...] + cw_ref[CONV_WIDTH - 1:CONV_WIDTH, :] * xl
    for j in range(CONV_WIDTH - 1):
        off = hist - (CONV_WIDTH - 1) + j
        xc = xc + cw_ref[j:j + 1, :] * xbuf[off:off + ts, :]
    xbuf[0:hist, :] = xbuf[ts:ts + hist, :]

    gates = jnp.dot(xc.astype(BF16), wg_ref[...], preferred_element_type=F32) + bg_ref[...]
    r = jax.nn.sigmoid(gates[:, :w])
    ig = jax.nn.sigmoid(gates[:, w:])
    lam = lam_ref[...]
    sp = jnp.log1p(jnp.exp(-jnp.abs(lam))) + jnp.maximum(-lam, 0.0)
    log_a = (-LRU_C) * r * sp
    a = jnp.exp(log_a)
    m2 = jnp.maximum(1.0 - a * a, 0.0)
    mult = jnp.where(m2 > 0.0, m2 * lax.rsqrt(m2), 0.0)
    b = mult * ig * xc

    rows = lax.broadcasted_iota(jnp.int32, (ts, w), 0) % SUBLANES
    d = 1
    while d < SUBLANES:
        keep = rows >= d
        a_prev = jnp.where(keep, pltpu.roll(a, d, 0), 1.0)
        b_prev = jnp.where(keep, pltpu.roll(b, d, 0), 0.0)
        b = a * b_prev + b
        a = a * a_prev
        d *= 2
    carry = hcar[0:1, :]
    groups = []
    for g in range(ts // SUBLANES):
        rs = slice(g * SUBLANES, (g + 1) * SUBLANES)
        hg = b[rs, :] + a[rs, :] * carry
        groups.append(hg)
        carry = hg[SUBLANES - 1:SUBLANES, :]
    h = jnp.concatenate(groups, axis=0)
    hcar[...] = jnp.broadcast_to(carry, hcar.shape)

    y = h * jax.nn.gelu(gate)
    o_ref[...] = _rms(y, gn_ref[...]).astype(o_ref.dtype)


def _in_lru(x2d, batch, g_mix, w_lru, w_qkv, conv_w, conv_b, w_gates, b_gates, lam, g_norm):
    T, D = x2d.shape
    w = conv_w.shape[-1]
    aw = w_qkv.shape[1] // 3
    ts = TOKEN_TILE
    nt = T // batch // ts
    row = lambda n: pl.BlockSpec((ts, n), lambda b, s: (b * nt + s, 0))
    return pl.pallas_call(
        _in_lru_kernel,
        grid=(batch, nt),
        in_specs=[row(D), _const_spec((1, D)), _const_spec(w_lru.shape),
                  _const_spec(w_qkv.shape), _const_spec(conv_w.shape), _const_spec((1, w)),
                  _const_spec(w_gates.shape), _const_spec((1, 2 * w)),
                  _const_spec((1, w)), _const_spec((1, w))],
        out_specs=[row(w), row(aw), row(aw), row(aw)],
        out_shape=[jax.ShapeDtypeStruct((T, w), BF16)]
        + [jax.ShapeDtypeStruct((T, aw), BF16)] * 3,
        scratch_shapes=[pltpu.VMEM((ts + SUBLANES, w), F32), pltpu.VMEM((SUBLANES, w), F32)],
        compiler_params=_params("parallel", "arbitrary"),
    )(x2d, g_mix, w_lru, w_qkv, conv_w, conv_b, w_gates, b_gates, lam, g_norm)


def _chunk_attn_kernel(q_ref, *refs):
    n = ATT_STEP_TILES
    k_refs, v_refs = refs[:n + 2], refs[n + 2:2 * n + 4]
    bias_ref, gn_ref, o_ref = refs[2 * n + 4:]
    aw = q_ref.shape[1]
    tq = k_refs[0].shape[0]
    dh = aw // ATT_HEADS
    scale = dh ** -0.5
    kw = 3 * tq
    assert LANES == 2 * dh
    col_blk = lax.broadcasted_iota(jnp.int32, (2 * tq, kw), 1) // tq
    first = lax.broadcasted_iota(jnp.int32, (1, LANES), 1) < dh
    for t in range(n):
        m = n * pl.program_id(1) + t
        valid = (col_blk + m) >= 2
        rows = slice(t * tq, (t + 1) * tq)
        outs = []
        for pair in range(aw // LANES):
            sl = slice(pair * LANES, (pair + 1) * LANES)
            qp = q_ref[rows, sl] * scale
            kp = jnp.concatenate([r[:, sl] for r in k_refs[t:t + 3]], axis=0)
            vp = jnp.concatenate([r[:, sl] for r in v_refs[t:t + 3]], axis=0)
            zero = jnp.zeros_like(qp)
            q2 = jnp.concatenate([jnp.where(first, qp, zero), jnp.where(first, zero, qp)],
                                 axis=0)
            s = lax.dot_general(q2, kp, (((1,), (1,)), ((), ())), preferred_element_type=F32)
            bias = bias_ref[2 * pair:2 * pair + 2].reshape(2 * tq, kw)
            s = jnp.where(valid, s + bias, NEG_INF)
            mx = jnp.max(s, axis=-1, keepdims=True)
            p = jnp.exp(s - mx).astype(BF16)
            ov = jnp.dot(p, jnp.concatenate([vp, jnp.ones_like(vp)], axis=1),
                         preferred_element_type=F32)
            o = ov[:, :LANES] * (1.0 / ov[:, LANES:LANES + 1])
            outs.append(jnp.where(first, o[:tq], o[tq:]))
        y = jnp.concatenate(outs, axis=-1)
        o_ref[rows, :] = _rms(y, gn_ref[...]).astype(o_ref.dtype)


def _attention_bias(rel_bias):
    tq = ATT_TILE
    kw = 3 * tq
    qpos = np.arange(tq)[:, None]
    kpos = np.arange(kw)[None, :] - 2 * tq
    qc = qpos // CHUNK
    kc = np.floor_divide(kpos, CHUNK)
    band = (kc <= qc) & (kc >= qc - LEFT_CHUNKS)
    period = 4 * tq
    j = np.arange(period)
    col_minus_row = np.where(j < kw, j, j - period)
    idx = np.clip(2 * tq - col_minus_row, -REL_CLIP, REL_CLIP) + REL_CLIP
    profile = rel_bias[:, idx].astype(F32)
    heads = profile.shape[0]
    sheared = jnp.tile(profile, (1, tq))[:, :tq * (period - 1)].reshape(heads, tq, period - 1)
    return jnp.where(band[None], sheared[:, :, :kw], NEG_INF)


def _chunk_attn(q, k, v, batch, bias, g_norm):
    T, aw = q.shape
    tq = ATT_TILE
    nt = T // batch // tq
    n = ATT_STEP_TILES
    steps = nt // n
    qspec = pl.BlockSpec((n * tq, aw), lambda b, m: (b * steps + m, 0))

    def kspec(j):
        return pl.BlockSpec((tq, aw),
                            lambda b, m: (b * nt + jnp.maximum(n * m - 2 + j, 0), 0))

    kv_specs = [kspec(j) for j in range(n + 2)]
    return pl.pallas_call(
        _chunk_attn_kernel,
        grid=(batch, steps),
        in_specs=[qspec] + kv_specs + kv_specs
        + [_const_spec(bias.shape), _const_spec((1, aw))],
        out_specs=qspec,
        out_shape=jax.ShapeDtypeStruct((T, aw), BF16),
        compiler_params=_params("parallel", "parallel"),
    )(q, *([k] * (n + 2)), *([v] * (n + 2)), bias, g_norm)


def _mem_kv_kernel(m_ref, g_ref, w_ref, k_ref, v_ref):
    mn = _rms(m_ref[...], g_ref[...]).astype(BF16)
    kv = jnp.dot(mn, w_ref[...], preferred_element_type=F32)
    d = k_ref.shape[-1]
    k_ref[...] = kv[:, :d].astype(BF16)
    v_ref[...] = kv[:, d:].astype(BF16)


def _mem_kv(mem2d, batch, g, w_kv):
    R, D = mem2d.shape
    ml = R // batch
    row = pl.BlockSpec((ml, D), lambda b: (b, 0))
    return pl.pallas_call(
        _mem_kv_kernel,
        grid=(batch,),
        in_specs=[row, _const_spec((1, D)), _const_spec(w_kv.shape)],
        out_specs=[row, row],
        out_shape=[jax.ShapeDtypeStruct((R, D), BF16)] * 2,
        compiler_params=_params("parallel"),
    )(mem2d, g, w_kv)


def _mix_cross_kernel(x_ref, ya_ref, yb_ref, woa_ref, wob_ref, gc_ref, wq_ref, km_ref,
                      vm_ref, wo_ref, gf_ref, wqp_ref, x2_ref, hnt_ref, qp_ref):
    x1 = (x_ref[...]
          + jnp.dot(ya_ref[...], woa_ref[...], preferred_element_type=F32)
          + jnp.dot(yb_ref[...], wob_ref[...], preferred_element_type=F32))
    hn2 = _rms(x1, gc_ref[...]).astype(BF16)
    q = jnp.dot(hn2, wq_ref[...], preferred_element_type=F32).astype(BF16)
    d = q.shape[-1]
    dh = d // MEM_HEADS
    scale = dh ** -0.5
    heads = []
    for h in range(MEM_HEADS):
        sl = slice(h * dh, (h + 1) * dh)
        s = lax.dot_general(q[:, sl], km_ref[:, sl], (((1,), (1,)), ((), ())),
                            preferred_element_type=F32) * scale
        mx = jnp.max(s, axis=-1, keepdims=True)
        p = jnp.exp(s - mx)
        l = jnp.sum(p, axis=-1, keepdims=True)
        o = jnp.dot(p.astype(BF16), vm_ref[:, sl], preferred_element_type=F32) * (1.0 / l)
        heads.append(o.astype(BF16))
    o_all = jnp.concatenate(heads, axis=-1)
    x2 = x1 + jnp.dot(o_all, wo_ref[...], preferred_element_type=F32)
    x2_ref[...] = x2
    hn3 = _rms(x2, gf_ref[...])
    hnt_ref[...] = hn3.T.astype(BF16)
    qp_ref[...] = jnp.dot(hn3.astype(BF16), wqp_ref[...],
                          preferred_element_type=F32).astype(BF16)


def _mix_cross(x2d, ya, yb, wo_a, wo_b, g_cross, w_q, kmem, vmem, w_o, g_ffn, w_qp, batch):
    T, D = x2d.shape
    tm = MIX_TILE
    per_batch = T // batch // tm
    ml = kmem.shape[0] // batch
    row = lambda w: pl.BlockSpec((tm, w), lambda i: (i, 0))
    mem_spec = pl.BlockSpec((ml, D), lambda i: (i // per_batch, 0))
    qw = w_qp.shape[1]
    return pl.pallas_call(
        _mix_cross_kernel,
        grid=(T // tm,),
        in_specs=[row(D), row(ya.shape[1]), row(yb.shape[1]),
                  _const_spec(wo_a.shape), _const_spec(wo_b.shape), _const_spec((1, D)),
                  _const_spec(w_q.shape), mem_spec, mem_spec, _const_spec(w_o.shape),
                  _const_spec((1, D)), _const_spec(w_qp.shape)],
        out_specs=[row(D), pl.BlockSpec((D, tm), lambda i: (0, i)), row(qw)],
        out_shape=[jax.ShapeDtypeStruct((T, D), F32), jax.ShapeDtypeStruct((D, T), BF16),
                   jax.ShapeDtypeStruct((T, qw), BF16)],
        compiler_params=_params("parallel"),
    )(x2d, ya, yb, wo_a, wo_b, g_cross, w_q, kmem, vmem, w_o, g_ffn, w_qp)


def _vmax(a, b):
    if a is None:
        return b
    if b is None:
        return a
    return jnp.maximum(a, b)


def _vmin(a, b):
    if a is None or b is None:
        return None
    return jnp.minimum(a, b)


def _bitonic_merge_desc(vals):
    n = len(vals)
    vals = list(vals)
    j = n // 2
    while j >= 1:
        for i in range(n):
            l = i ^ j
            if l > i:
                hi, lo = _vmax(vals[i], vals[l]), _vmin(vals[i], vals[l])
                vals[i], vals[l] = hi, lo
        j //= 2
    return vals


def _bitonic_sort_desc(vals):
    n = len(vals)
    vals = list(vals)
    k = 2
    while k <= n:
        j = k // 2
        while j >= 1:
            for i in range(n):
                l = i ^ j
                if l > i:
                    hi, lo = _vmax(vals[i], vals[l]), _vmin(vals[i], vals[l])
                    if (i & k) == 0:
                        vals[i], vals[l] = hi, lo
                    else:
                        vals[i], vals[l] = lo, hi
            j //= 2
        k *= 2
    return vals


def _top16_desc(sc):
    n = sc.shape[-1]
    k = PEER_TOPK
    sc3 = sc.reshape(N_KEYS // SUBLANES, SUBLANES, n)
    vals = _bitonic_sort_desc([sc3[i] for i in range(k)])
    shift = SUBLANES // 2
    while shift >= 1:
        other = [pltpu.roll(v, shift, 0) for v in vals]
        vals = _bitonic_merge_desc([jnp.maximum(vals[i], other[k - 1 - i]) for i in range(k)])
        shift //= 2
    return vals


def _count_prefix16(test, rows):
    t8 = test(rows[7])
    t4 = test(jnp.where(t8, rows[11], rows[3]))
    lo = jnp.where(t4, rows[5], rows[1])
    hi = jnp.where(t4, rows[13], rows[9])
    t2 = test(jnp.where(t8, hi, lo))
    quad = [jnp.where(t2, rows[4 * q + 2], rows[4 * q]) for q in range(4)]
    lo = jnp.where(t4, quad[1], quad[0])
    hi = jnp.where(t4, quad[3], quad[2])
    t1 = test(jnp.where(t8, hi, lo))
    t16 = test(rows[15])
    count = jnp.where(t8, 8.0, 0.0) + jnp.where(t4, 4.0, 0.0)
    count = count + jnp.where(t2, 2.0, 0.0) + jnp.where(t1, 1.0, 0.0)
    return count + jnp.where(t16, 1.0, 0.0)


_CAND = [(a, b) for a in range(PEER_TOPK) for b in range(PEER_TOPK)
         if (a + 1) * (b + 1) <= PEER_TOPK]


def _peer_route_kernel(qp_ref, keys_ref, nsel_ref, c1_ref, rank2_ref, e2_ref):
    tn = qp_ref.shape[0]
    k = PEER_TOPK
    kd = keys_ref.shape[-1]
    sub = lax.broadcasted_iota(jnp.int32, (SUBLANES, tn), 0)

    scores = []
    tops = []
    for hp in range(2 * PEER_HEADS):
        qh = qp_ref[:, hp * kd:(hp + 1) * kd]
        sc = lax.dot_general(keys_ref[hp], qh, (((1,), (1,)), ((), ())),
                             preferred_element_type=F32)
        scores.append(sc)
        tops.append(_top16_desc(sc))

    def pack(p, a):
        out = tops[p][a]
        for h in range(1, PEER_HEADS):
            out = jnp.where(sub == h, tops[2 * h + p][a], out)
        return out

    V1 = [pack(0, a) for a in range(k)]
    V2 = [pack(1, b) for b in range(k)]
    cand = {ab: V1[ab[0]] + V2[ab[1]] for ab in _CAND}
    padded = [cand[ab] for ab in _CAND] + [None] * (64 - len(_CAND))
    tau = _bitonic_sort_desc(padded)[k - 1]
    E1 = [jnp.exp(V1[a] - V1[0]) for a in range(k)]
    E2 = [jnp.exp(V2[b] - V2[0]) for b in range(k)]
    zsum = jnp.zeros((SUBLANES, tn), F32)
    for (a, b) in _CAND:
        zsum = zsum + jnp.where(cand[(a, b)] >= tau, E1[a] * E2[b], 0.0)
    zinv = 1.0 / zsum

    for h in range(PEER_HEADS):
        s1 = scores[2 * h]
        s2 = scores[2 * h + 1]
        slab = 2 * SUBLANES
        row = lambda arr: jnp.broadcast_to(arr[h:h + 1, :], (slab, tn))
        v2_rows = [row(V2[b]) for b in range(k)]
        tau_row = row(tau)
        v1_max, z_inv = row(V1[0]), row(zinv)
        for r in range(N_KEYS // slab):
            rs = slice(r * slab, (r + 1) * slab)
            s1r, s2r = s1[rs, :], s2[rs, :]
            nsel = _count_prefix16(lambda t: (s1r + t) >= tau_row, v2_rows)
            rank2 = _count_prefix16(lambda t: t > s2r, v2_rows)
            nsel_ref[h, rs, :] = nsel
            c1_ref[h, rs, :] = jnp.exp(s1r - v1_max) * z_inv
            rank2_ref[h, rs, :] = rank2.astype(rank2_ref.dtype)
            e2_ref[h, rs, :] = jnp.exp(s2r - v2_rows[0]).astype(e2_ref.dtype)


def _peer_route(qp, keys):
    T = qp.shape[0]
    tn = ROUTE_TILE
    head_major = pl.BlockSpec((PEER_HEADS, N_KEYS, tn), lambda i: (0, 0, i))
    return pl.pallas_call(
        _peer_route_kernel,
        grid=(T // tn,),
        in_specs=[pl.BlockSpec((tn, qp.shape[1]), lambda i: (i, 0)), _const_spec(keys.shape)],
        out_specs=[head_major] * 4,
        out_shape=[jax.ShapeDtypeStruct((PEER_HEADS, N_KEYS, T), F32)] * 2
        + [jax.ShapeDtypeStruct((PEER_HEADS, N_KEYS, T), BF16)] * 2,
        compiler_params=_params("parallel"),
    )(qp, keys)


_GELU_C = float(np.sqrt(2.0 / np.pi))


def _gelu_tanh(x):
    log2e = float(np.log2(np.e))
    k1 = -2.0 * _GELU_C * log2e
    k3 = -2.0 * _GELU_C * 0.044715 * log2e
    e = jnp.exp2(x * (k1 + k3 * (x * x)))
    return x / (1.0 + e)


def _rows_bf16(row, n):
    packed = 2 * SUBLANES
    tile = jnp.broadcast_to(row, (packed, row.shape[-1])).astype(BF16)
    return jnp.tile(tile, (n // packed, 1))


def _peer_dense_kernel(hnt_ref, ufirst_ref, unext_ref, vt_ref, nsel_ref, c1_ref, rank2_ref,
                       e2_ref, x2_ref, gn_ref, o_ref, acc_ref, act_ref, *, final_norm):
    j = pl.program_id(1)
    te = unext_ref.shape[0]

    @pl.when(j == 0)
    def _():
        acc_ref[...] = jnp.zeros_like(acc_ref)

    ch = act_ref.shape[1]
    groups = ch // N_KEYS
    nch = te // ch

    def up(rows):
        return jnp.dot(rows, hnt_ref[...], preferred_element_type=F32)

    def rows_after(c):
        unshifted = (j == pl.num_programs(1) - 1).astype(jnp.int32)
        local = jnp.minimum(c + unshifted, nch - 1)
        return unext_ref[pl.ds(pl.multiple_of(local * ch, ch), ch), :].astype(BF16)

    def down(c, slot):
        act = act_ref[slot]
        hs = []
        for g in range(groups):
            i1 = j * (te // N_KEYS) + c * groups + g
            gate = None
            for h in range(PEER_HEADS):
                n_b = _rows_bf16(nsel_ref[h, pl.ds(i1, 1), :], N_KEYS)
                c_b = _rows_bf16(c1_ref[h, pl.ds(i1, 1), :], N_KEYS)
                term = jnp.where(rank2_ref[h] < n_b, e2_ref[h], jnp.zeros_like(c_b)) * c_b
                gate = term if gate is None else gate + term
            es = slice(g * N_KEYS, (g + 1) * N_KEYS)
            hs.append(gate * _gelu_tanh(act[es, :]).astype(BF16))
        hmat = jnp.concatenate(hs, axis=0)
        return jnp.dot(vt_ref[c], hmat, preferred_element_type=F32)

    @pl.when(j == 0)
    def _():
        act_ref[0] = up(ufirst_ref[...].astype(BF16))

    def pair(k, carry):
        parts = []
        for slot in range(2):
            c = 2 * k + slot
            act_ref[1 - slot] = up(rows_after(c))
            parts.append(down(c, slot))
        acc_ref[...] += parts[0] + parts[1]
        return carry

    lax.fori_loop(0, nch // 2, pair, 0)

    @pl.when(j == pl.num_programs(1) - 1)
    def _():
        y = x2_ref[...] + acc_ref[...].T
        o_ref[...] = _rms(y, gn_ref[...]) if final_norm else y


def _peer_dense(hnt, u, v, nsel, c1, rank2, e2, x2, g_final, final_norm):
    D, T = hnt.shape
    E = u.shape[0]
    tm, te, ch = DENSE_TOKEN_TILE, DENSE_EXPERT_TILE, DENSE_CHUNK
    head_major = pl.BlockSpec((PEER_HEADS, N_KEYS, tm), lambda i, j: (0, 0, i))
    row = pl.BlockSpec((tm, D), lambda i, j: (i, 0))
    vt = v.astype(BF16).reshape(E // ch, ch, D).transpose(0, 2, 1)
    u_window = pl.BlockSpec((pl.Element(te), pl.Element(D)),
                            lambda i, j: (jnp.minimum(j * (te // ch) + 1, (E - te) // ch) * ch, 0))
    return pl.pallas_call(
        functools.partial(_peer_dense_kernel, final_norm=final_norm),
        grid=(T // tm, E // te),
        in_specs=[pl.BlockSpec((D, tm), lambda i, j: (0, i)),
                  pl.BlockSpec((ch, D), lambda i, j: (0, 0)),
                  u_window,
                  pl.BlockSpec((te // ch, D, ch), lambda i, j: (j, 0, 0)),
                  head_major, head_major, head_major, head_major, row,
                  pl.BlockSpec((1, D), lambda i, j: (0, 0))],
        out_specs=row,
        out_shape=jax.ShapeDtypeStruct((T, D), F32),
        scratch_shapes=[pltpu.VMEM((D, tm), F32), pltpu.VMEM((2, ch, tm), F32)],
        compiler_params=_params("parallel", "arbitrary"),
    )(hnt, u, u, vt, nsel, c1, rank2, e2, x2, g_final)


def _block_diag(w):
    nb, bi, bo = w.shape
    eye = jnp.eye(nb, dtype=w.dtype)
    return (eye[:, None, :, None] * w[:, :, None, :]).reshape(nb * bi, nb * bo)


def kernel(x, mem, norm_mix, w_in, conv_w, conv_b, gate_a_w, gate_a_b, gate_x_w, gate_x_b, lru_lambda, rel_bias, norm_grp_a, norm_grp_b, w_out, norm_cross, norm_mem, w_q_mem, w_kv_mem, w_o_mem, norm_ffn, w_query, sub_keys, expert_u, expert_v, norm_final):
    B, S, D = x.shape
    depth = w_in.shape[0]
    lru_w = conv_w.shape[-1]
    att_w = (w_in.shape[-1] - 2 * lru_w) // 3
    row = lambda v: v.reshape(1, -1)

    cur = x.reshape(B * S, D)
    mem2d = mem.reshape(B * mem.shape[1], D)
    for l in range(depth):
        w_gates = jnp.concatenate([_block_diag(gate_a_w[l]), _block_diag(gate_x_w[l])],
                                  axis=1).astype(BF16)
        b_gates = jnp.concatenate([gate_a_b[l], gate_x_b[l]]).reshape(1, -1)
        w_in16 = w_in[l].astype(BF16)
        ya, q, k, v = _in_lru(cur, B, row(norm_mix[l]), w_in16[:, :2 * lru_w],
                              w_in16[:, 2 * lru_w:], conv_w[l], row(conv_b[l]), w_gates,
                              b_gates, row(lru_lambda[l]), row(norm_grp_a[l]))
        yb = _chunk_attn(q, k, v, B, _attention_bias(rel_bias[l]), row(norm_grp_b[l]))
        kmem, vmem = _mem_kv(mem2d, B, row(norm_mem[l]), w_kv_mem[l].astype(BF16))
        wo = w_out[l].astype(BF16)
        x2, hn3, qp = _mix_cross(cur, ya, yb, wo[:lru_w], wo[lru_w:], row(norm_cross[l]),
                                 w_q_mem[l].astype(BF16), kmem, vmem,
                                 w_o_mem[l].astype(BF16), row(norm_ffn[l]),
                                 w_query[l].astype(BF16), B)
        keys = sub_keys[l].reshape(2 * PEER_HEADS, N_KEYS, -1).astype(BF16)
        nsel, c1, rank2, e2 = _peer_route(qp, keys)
        cur = _peer_dense(hn3, expert_u[l], expert_v[l], nsel, c1, rank2, e2, x2,
                          row(norm_final), l == depth - 1)
    return cur.reshape(B, S, D)
```

```python
import functools

import jax
import jax.numpy as jnp
import numpy as np
from jax import lax
from jax.experimental import pallas as pl
from jax.experimental.pallas import tpu as pltpu

F32 = jnp.float32
BF16 = jnp.bfloat16

EPS = 1e-6
NEG_INF = -1e30

CHUNK = 64
LEFT_CHUNKS = 8
REL_CLIP = 128
CONV_WIDTH = 4
LRU_C = 8.0
ATT_HEADS = 8
MEM_HEADS = 4
PEER_HEADS = 8
N_KEYS = 128
PEER_TOPK = 16

SUBLANES = 8
LANES = 128
VMEM_LIMIT_BYTES = 56 * 1024 * 1024

TOKEN_TILE = 1024
MIX_TILE = 1024
ATT_TILE = 4 * CHUNK
ATT_STEP_TILES = 8
ROUTE_TILE = 256
DENSE_TOKEN_TILE = 512
DENSE_EXPERT_TILE = 2048
DENSE_CHUNK = 512


def _params(*semantics, flags=None):
    return pltpu.CompilerParams(dimension_semantics=semantics,
                                vmem_limit_bytes=VMEM_LIMIT_BYTES, flags=flags)


def _rms(xf, g):
    return xf * lax.rsqrt(jnp.mean(xf * xf, axis=-1, keepdims=True) + EPS) * g


def _const_spec(shape):
    nd = len(shape)
    return pl.BlockSpec(shape, lambda *_: (0,) * nd, pipeline_mode=pl.Buffered(1))


def _in_lru_kernel(x_ref, g_ref, wl_ref, wqkv_ref, cw_ref, cb_ref, wg_ref, bg_ref, lam_ref,
                   gn_ref, o_ref, q_ref, k_ref, v_ref, xbuf, hcar):
    ts, w = o_ref.shape
    aw = q_ref.shape[-1]
    hist = SUBLANES

    @pl.when(pl.program_id(1) == 0)
    def _():
        xbuf[0:hist, :] = jnp.zeros((hist, w), F32)
        hcar[...] = jnp.zeros_like(hcar)

    hn = _rms(x_ref[...], g_ref[...]).astype(BF16)
    zl = jnp.dot(hn, wl_ref[...], preferred_element_type=F32)

    xl = zl[:, :w]
    gate = zl[:, w:]
    xbuf[hist:hist + ts, :] = xl
    xc = cb_ref[...] + cw_ref[CONV_WIDTH - 1:CONV_WIDTH, :] * xl
    for j in range(CONV_WIDTH - 1):
        off = hist - (CONV_WIDTH - 1) + j
        xc = xc + cw_ref[j:j + 1, :] * xbuf[off:off + ts, :]
    xbuf[0:hist, :] = xbuf[ts:ts + hist, :]

    gates = jnp.dot(xc.astype(BF16), wg_ref[...], preferred_element_type=F32) + bg_ref[...]
    r = jax.nn.sigmoid(gates[:, :w])
    ig = jax.nn.sigmoid(gates[:, w:])
    lam = lam_ref[...]
    sp = jnp.log1p(jnp.exp(-jnp.abs(lam))) + jnp.maximum(-lam, 0.0)
    log_a = (-LRU_C) * r * sp
    a = jnp.exp(log_a)
    m2 = jnp.maximum(1.0 - a * a, 0.0)
    mult = jnp.where(m2 > 0.0, m2 * lax.rsqrt(m2), 0.0)
    b = mult * ig * xc

    qkv = jnp.dot(hn, wqkv_ref[...], preferred_element_type=F32)
    q_ref[...] = qkv[:, :aw].astype(BF16)
    k_ref[...] = qkv[:, aw:2 * aw].astype(BF16)
    v_ref[...] = qkv[:, 2 * aw:].astype(BF16)

    rows = lax.broadcasted_iota(jnp.int32, (ts, w), 0) % SUBLANES
    d = 1
    while d < SUBLANES:
        keep = rows >= d
        a_prev = jnp.where(keep, pltpu.roll(a, d, 0), 1.0)
        b_prev = jnp.where(keep, pltpu.roll(b, d, 0), 0.0)
        b = a * b_prev + b
        a = a * a_prev
        d *= 2
    carry = hcar[0:1, :]
    groups = []
    for g in range(ts // SUBLANES):
        rs = slice(g * SUBLANES, (g + 1) * SUBLANES)
        hg = b[rs, :] + a[rs, :] * carry
        groups.append(hg)
        carry = hg[SUBLANES - 1:SUBLANES, :]
    h = jnp.concatenate(groups, axis=0)
    hcar[...] = jnp.broadcast_to(carry, hcar.shape)

    y = h * jax.nn.gelu(gate)
    o_ref[...] = _rms(y, gn_ref[...]).astype(o_ref.dtype)


def _in_lru(x2d, batch, g_mix, w_lru, w_qkv, conv_w, conv_b, w_gates, b_gates, lam, g_norm):
    T, D = x2d.shape
    w = conv_w.shape[-1]
    aw = w_qkv.shape[1] // 3
    ts = TOKEN_TILE
    nt = T // batch // ts
    row = lambda n: pl.BlockSpec((ts, n), lambda b, s: (b * nt + s, 0))
    return pl.pallas_call(
        _in_lru_kernel,
        grid=(batch, nt),
        in_specs=[row(D), _const_spec((1, D)), _const_spec(w_lru.shape),
                  _const_spec(w_qkv.shape), _const_spec(conv_w.shape), _const_spec((1, w)),
                  _const_spec(w_gates.shape), _const_spec((1, 2 * w)),
                  _const_spec((1, w)), _const_spec((1, w))],
        out_specs=[row(w), row(aw), row(aw), row(aw)],
        out_shape=[jax.ShapeDtypeStruct((T, w), BF16)]
        + [jax.ShapeDtypeStruct((T, aw), BF16)] * 3,
        scratch_shapes=[pltpu.VMEM((ts + SUBLANES, w), F32), pltpu.VMEM((SUBLANES, w), F32)],
        compiler_params=_params("parallel", "arbitrary"),
    )(x2d, g_mix, w_lru, w_qkv, conv_w, conv_b, w_gates, b_gates, lam, g_norm)


def _chunk_attn_kernel(q_ref, *refs):
    n = ATT_STEP_TILES
    k_refs, v_refs = refs[:n + 2], refs[n + 2:2 * n + 4]
    bias_ref, gn_ref, o_ref = refs[2 * n + 4:]
    aw = q_ref.shape[1]
    tq = k_refs[0].shape[0]
    dh = aw // ATT_HEADS
    scale = dh ** -0.5
    kw = 3 * tq
    assert LANES == 2 * dh
    col_blk = lax.broadcasted_iota(jnp.int32, (2 * tq, kw), 1) // tq
    first = lax.broadcasted_iota(jnp.int32, (1, LANES), 1) < dh
    for t in range(n):
        m = n * pl.program_id(1) + t
        valid = (col_blk + m) >= 2
        rows = slice(t * tq, (t + 1) * tq)
        outs = []
        for pair in range(aw // LANES):
            sl = slice(pair * LANES, (pair + 1) * LANES)
            qp = q_ref[rows, sl] * scale
            kp = jnp.concatenate([r[:, sl] for r in k_refs[t:t + 3]], axis=0)
            vp = jnp.concatenate([r[:, sl] for r in v_refs[t:t + 3]], axis=0)
            zero = jnp.zeros_like(qp)
            q2 = jnp.concatenate([jnp.where(first, qp, zero), jnp.where(first, zero, qp)],
                                 axis=0)
            s = lax.dot_general(q2, kp, (((1,), (1,)), ((), ())), preferred_element_type=F32)
            bias = bias_ref[2 * pair:2 * pair + 2].reshape(2 * tq, kw)
            s = jnp.where(valid, s + bias, NEG_INF)
            mx = jnp.max(s, axis=-1, keepdims=True)
            p = jnp.exp(s - mx).astype(BF16)
            ov = jnp.dot(p, jnp.concatenate([vp, jnp.ones_like(vp)], axis=1),
                         preferred_element_type=F32)
            o = ov[:, :LANES] * (1.0 / ov[:, LANES:LANES + 1])
            outs.append(jnp.where(first, o[:tq], o[tq:]))
        y = jnp.concatenate(outs, axis=-1)
        o_ref[rows, :] = _rms(y, gn_ref[...]).astype(o_ref.dtype)


def _attention_bias(rel_bias):
    tq = ATT_TILE
    kw = 3 * tq
    qpos = np.arange(tq)[:, None]
    kpos = np.arange(kw)[None, :] - 2 * tq
    qc = qpos // CHUNK
    kc = np.floor_divide(kpos, CHUNK)
    band = (kc <= qc) & (kc >= qc - LEFT_CHUNKS)
    period = 4 * tq
    j = np.arange(period)
    col_minus_row = np.where(j < kw, j, j - period)
    idx = np.clip(2 * tq - col_minus_row, -REL_CLIP, REL_CLIP) + REL_CLIP
    profile = rel_bias[:, idx].astype(F32)
    heads = profile.shape[0]
    sheared = jnp.tile(profile, (1, tq))[:, :tq * (period - 1)].reshape(heads, tq, period - 1)
    return jnp.where(band[None], sheared[:, :, :kw], NEG_INF)


def _chunk_attn(q, k, v, batch, bias, g_norm):
    T, aw = q.shape
    tq = ATT_TILE
    nt = T // batch // tq
    n = ATT_STEP_TILES
    steps = nt // n
    qspec = pl.BlockSpec((n * tq, aw), lambda b, m: (b * steps + m, 0))

    def kspec(j):
        return pl.BlockSpec((tq, aw),
                            lambda b, m: (b * nt + jnp.maximum(n * m - 2 + j, 0), 0))

    kv_specs = [kspec(j) for j in range(n + 2)]
    return pl.pallas_call(
        _chunk_attn_kernel,
        grid=(batch, steps),
        in_specs=[qspec] + kv_specs + kv_specs
        + [_const_spec(bias.shape), _const_spec((1, aw))],
        out_specs=qspec,
        out_shape=jax.ShapeDtypeStruct((T, aw), BF16),
        compiler_params=_params("parallel", "parallel"),
    )(q, *([k] * (n + 2)), *([v] * (n + 2)), bias, g_norm)


def _mem_kv_kernel(m_ref, g_ref, w_ref, k_ref, v_ref):
    mn = _rms(m_ref[...], g_ref[...]).astype(BF16)
    kv = jnp.dot(mn, w_ref[...], preferred_element_type=F32)
    d = k_ref.shape[-1]
    k_ref[...] = kv[:, :d].astype(BF16)
    v_ref[...] = kv[:, d:].astype(BF16)


def _mem_kv(mem2d, batch, g, w_kv):
    R, D = mem2d.shape
    ml = R // batch
    row = pl.BlockSpec((ml, D), lambda b: (b, 0))
    return pl.pallas_call(
        _mem_kv_kernel,
        grid=(batch,),
        in_specs=[row, _const_spec((1, D)), _const_spec(w_kv.shape)],
        out_specs=[row, row],
        out_shape=[jax.ShapeDtypeStruct((R, D), BF16)] * 2,
        compiler_params=_params("parallel"),
    )(mem2d, g, w_kv)


def _mix_cross_kernel(x_ref, ya_ref, yb_ref, woa_ref, wob_ref, gc_ref, wq_ref, km_ref,
                      vm_ref, wo_ref, gf_ref, wqp_ref, x2_ref, hnt_ref, qp_ref):
    x1 = (x_ref[...]
          + jnp.dot(ya_ref[...], woa_ref[...], preferred_element_type=F32)
          + jnp.dot(yb_ref[...], wob_ref[...], preferred_element_type=F32))
    hn2 = _rms(x1, gc_ref[...]).astype(BF16)
    q = jnp.dot(hn2, wq_ref[...], preferred_element_type=F32).astype(BF16)
    d = q.shape[-1]
    dh = d // MEM_HEADS
    scale = dh ** -0.5
    heads = []
    for h in range(MEM_HEADS):
        sl = slice(h * dh, (h + 1) * dh)
        s = lax.dot_general(q[:, sl], km_ref[:, sl], (((1,), (1,)), ((), ())),
                            preferred_element_type=F32) * scale
        mx = jnp.max(s, axis=-1, keepdims=True)
        p = jnp.exp(s - mx)
        l = jnp.sum(p, axis=-1, keepdims=True)
        o = jnp.dot(p.astype(BF16), vm_ref[:, sl], preferred_element_type=F32) * (1.0 / l)
        heads.append(o.astype(BF16))
    o_all = jnp.concatenate(heads, axis=-1)
    x2 = x1 + jnp.dot(o_all, wo_ref[...], preferred_element_type=F32)
    x2_ref[...] = x2
    hn3 = _rms(x2, gf_ref[...])
    hnt_ref[...] = hn3.T.astype(BF16)
    qp_ref[...] = jnp.dot(hn3.astype(BF16), wqp_ref[...],
                          preferred_element_type=F32).astype(BF16)


def _mix_cross(x2d, ya, yb, wo_a, wo_b, g_cross, w_q, kmem, vmem, w_o, g_ffn, w_qp, batch):
    T, D = x2d.shape
    tm = MIX_TILE
    per_batch = T // batch // tm
    ml = kmem.shape[0] // batch
    row = lambda w: pl.BlockSpec((tm, w), lambda i: (i, 0))
    mem_spec = pl.BlockSpec((ml, D), lambda i: (i // per_batch, 0))
    qw = w_qp.shape[1]
    return pl.pallas_call(
        _mix_cross_kernel,
        grid=(T // tm,),
        in_specs=[row(D), row(ya.shape[1]), row(yb.shape[1]),
                  _const_spec(wo_a.shape), _const_spec(wo_b.shape), _const_spec((1, D)),
                  _const_spec(w_q.shape), mem_spec, mem_spec, _const_spec(w_o.shape),
                  _const_spec((1, D)), _const_spec(w_qp.shape)],
        out_specs=[row(D), pl.BlockSpec((D, tm), lambda i: (0, i)), row(qw)],
        out_shape=[jax.ShapeDtypeStruct((T, D), F32), jax.ShapeDtypeStruct((D, T), BF16),
                   jax.ShapeDtypeStruct((T, qw), BF16)],
        compiler_params=_params("parallel"),
    )(x2d, ya, yb, wo_a, wo_b, g_cross, w_q, kmem, vmem, w_o, g_ffn, w_qp)


def _vmax(a, b):
    if a is None:
        return b
    if b is None:
        return a
    return jnp.maximum(a, b)


def _vmin(a, b):
    if a is None or b is None:
        return None
    return jnp.minimum(a, b)


def _bitonic_merge_desc(vals):
    n = len(vals)
    vals = list(vals)
    j = n // 2
    while j >= 1:
        for i in range(n):
            l = i ^ j
            if l > i:
                hi, lo = _vmax(vals[i], vals[l]), _vmin(vals[i], vals[l])
                vals[i], vals[l] = hi, lo
        j //= 2
    return vals


def _bitonic_sort_desc(vals):
    n = len(vals)
    vals = list(vals)
    k = 2
    while k <= n:
        j = k // 2
        while j >= 1:
            for i in range(n):
                l = i ^ j
                if l > i:
                    hi, lo = _vmax(vals[i], vals[l]), _vmin(vals[i], vals[l])
                    if (i & k) == 0:
                        vals[i], vals[l] = hi, lo
                    else:
                        vals[i], vals[l] = lo, hi
            j //= 2
        k *= 2
    return vals


def _top16_desc(sc):
    n = sc.shape[-1]
    k = PEER_TOPK
    sc3 = sc.reshape(N_KEYS // SUBLANES, SUBLANES, n)
    vals = _bitonic_sort_desc([sc3[i] for i in range(k)])
    shift = SUBLANES // 2
    while shift >= 1:
        other = [pltpu.roll(v, shift, 0) for v in vals]
        vals = _bitonic_merge_desc([jnp.maximum(vals[i], other[k - 1 - i]) for i in range(k)])
        shift //= 2
    return vals


def _count_prefix16(test, rows):
    t8 = test(rows[7])
    t4 = test(jnp.where(t8, rows[11], rows[3]))
    lo = jnp.where(t4, rows[5], rows[1])
    hi = jnp.where(t4, rows[13], rows[9])
    t2 = test(jnp.where(t8, hi, lo))
    quad = [jnp.where(t2, rows[4 * q + 2], rows[4 * q]) for q in range(4)]
    lo = jnp.where(t4, quad[1], quad[0])
    hi = jnp.where(t4, quad[3], quad[2])
    t1 = test(jnp.where(t8, hi, lo))
    t16 = test(rows[15])
    count = jnp.where(t8, 8.0, 0.0) + jnp.where(t4, 4.0, 0.0)
    count = count + jnp.where(t2, 2.0, 0.0) + jnp.where(t1, 1.0, 0.0)
    return count + jnp.where(t16, 1.0, 0.0)


_CAND = [(a, b) for a in range(PEER_TOPK) for b in range(PEER_TOPK)
         if (a + 1) * (b + 1) <= PEER_TOPK]


def _peer_route_kernel(qp_ref, keys_ref, nsel_ref, c1_ref, rank2_ref, e2_ref):
    tn = qp_ref.shape[0]
    k = PEER_TOPK
    kd = keys_ref.shape[-1]
    sub = lax.broadcasted_iota(jnp.int32, (SUBLANES, tn), 0)

    scores = []
    tops = []
    for hp in range(2 * PEER_HEADS):
        qh = qp_ref[:, hp * kd:(hp + 1) * kd]
        sc = lax.dot_general(keys_ref[hp], qh, (((1,), (1,)), ((), ())),
                             preferred_element_type=F32)
        scores.append(sc)
        tops.append(_top16_desc(sc))

    def pack(p, a):
        out = tops[p][a]
        for h in range(1, PEER_HEADS):
            out = jnp.where(sub == h, tops[2 * h + p][a], out)
        return out

    V1 = [pack(0, a) for a in range(k)]
    V2 = [pack(1, b) for b in range(k)]
    cand = {ab: V1[ab[0]] + V2[ab[1]] for ab in _CAND}
    padded = [cand[ab] for ab in _CAND] + [None] * (64 - len(_CAND))
    tau = _bitonic_sort_desc(padded)[k - 1]
    E1 = [jnp.exp(V1[a] - V1[0]) for a in range(k)]
    E2 = [jnp.exp(V2[b] - V2[0]) for b in range(k)]
    zsum = jnp.zeros((SUBLANES, tn), F32)
    for (a, b) in _CAND:
        zsum = zsum + jnp.where(cand[(a, b)] >= tau, E1[a] * E2[b], 0.0)
    zinv = 1.0 / zsum

    for h in range(PEER_HEADS):
        s1 = scores[2 * h]
        s2 = scores[2 * h + 1]
        slab = 2 * SUBLANES
        row = lambda arr: jnp.broadcast_to(arr[h:h + 1, :], (slab, tn))
        v2_rows = [row(V2[b]) for b in range(k)]
        tau_row = row(tau)
        v1_max, z_inv = row(V1[0]), row(zinv)
        for r in range(N_KEYS // slab):
            rs = slice(r * slab, (r + 1) * slab)
            s1r, s2r = s1[rs, :], s2[rs, :]
            nsel = _count_prefix16(lambda t: (s1r + t) >= tau_row, v2_rows)
            rank2 = _count_prefix16(lambda t: t > s2r, v2_rows)
            nsel_ref[h, rs, :] = nsel
            c1_ref[h, rs, :] = jnp.exp(s1r - v1_max) * z_inv
            rank2_ref[h, rs, :] = rank2.astype(rank2_ref.dtype)
            e2_ref[h, rs, :] = jnp.exp(s2r - v2_rows[0]).astype(e2_ref.dtype)


def _peer_route(qp, keys):
    T = qp.shape[0]
    tn = ROUTE_TILE
    head_major = pl.BlockSpec((PEER_HEADS, N_KEYS, tn), lambda i: (0, 0, i))
    return pl.pallas_call(
        _peer_route_kernel,
        grid=(T // tn,),
        in_specs=[pl.BlockSpec((tn, qp.shape[1]), lambda i: (i, 0)), _const_spec(keys.shape)],
        out_specs=[head_major] * 4,
        out_shape=[jax.ShapeDtypeStruct((PEER_HEADS, N_KEYS, T), F32)] * 2
        + [jax.ShapeDtypeStruct((PEER_HEADS, N_KEYS, T), BF16)] * 2,
        compiler_params=_params("parallel"),
    )(qp, keys)


_GELU_C = float(np.sqrt(2.0 / np.pi))


def _gelu_tanh(x):
    log2e = float(np.log2(np.e))
    k1 = -2.0 * _GELU_C * log2e
    k3 = -2.0 * _GELU_C * 0.044715 * log2e
    e = jnp.exp2(x * (k1 + k3 * (x * x)))
    return x / (1.0 + e)


def _rows_bf16(row, n):
    packed = 2 * SUBLANES
    tile = jnp.broadcast_to(row, (packed, row.shape[-1])).astype(BF16)
    return jnp.tile(tile, (n // packed, 1))


def _peer_dense_kernel(hnt_ref, ufirst_ref, unext_ref, vt_ref, nsel_ref, c1_ref, rank2_ref,
                       e2_ref, x2_ref, gn_ref, o_ref, acc_ref, act_ref, *, final_norm):
    j = pl.program_id(1)
    te = unext_ref.shape[0]

    @pl.when(j == 0)
    def _():
        acc_ref[...] = jnp.zeros_like(acc_ref)

    ch = act_ref.shape[1]
    groups = ch // N_KEYS
    nch = te // ch

    def up(rows):
        return jnp.dot(rows, hnt_ref[...], preferred_element_type=F32)

    def rows_after(c):
        unshifted = (j == pl.num_programs(1) - 1).astype(jnp.int32)
        local = jnp.minimum(c + unshifted, nch - 1)
        return unext_ref[pl.ds(pl.multiple_of(local * ch, ch), ch), :].astype(BF16)

    def down(c, slot):
        act = act_ref[slot]
        hs = []
        for g in range(groups):
            i1 = j * (te // N_KEYS) + c * groups + g
            gate = None
            for h in range(PEER_HEADS):
                n_b = _rows_bf16(nsel_ref[h, pl.ds(i1, 1), :], N_KEYS)
                c_b = _rows_bf16(c1_ref[h, pl.ds(i1, 1), :], N_KEYS)
                term = jnp.where(rank2_ref[h] < n_b, e2_ref[h], jnp.zeros_like(c_b)) * c_b
                gate = term if gate is None else gate + term
            es = slice(g * N_KEYS, (g + 1) * N_KEYS)
            hs.append(gate * _gelu_tanh(act[es, :]).astype(BF16))
        hmat = jnp.concatenate(hs, axis=0)
        return jnp.dot(vt_ref[c], hmat, preferred_element_type=F32)

    @pl.when(j == 0)
    def _():
        act_ref[0] = up(ufirst_ref[...].astype(BF16))

    def pair(k, carry):
        parts = []
        for slot in range(2):
            c = 2 * k + slot
            act_ref[1 - slot] = up(rows_after(c))
            parts.append(down(c, slot))
        acc_ref[...] += parts[0] + parts[1]
        return carry

    lax.fori_loop(0, nch // 2, pair, 0)

    @pl.when(j == pl.num_programs(1) - 1)
    def _():
        y = x2_ref[...] + acc_ref[...].T
        o_ref[...] = _rms(y, gn_ref[...]) if final_norm else y


def _peer_dense(hnt, u, v, nsel, c1, rank2, e2, x2, g_final, final_norm):
    D, T = hnt.shape
    E = u.shape[0]
    tm, te, ch = DENSE_TOKEN_TILE, DENSE_EXPERT_TILE, DENSE_CHUNK
    head_major = pl.BlockSpec((PEER_HEADS, N_KEYS, tm), lambda i, j: (0, 0, i))
    row = pl.BlockSpec((tm, D), lambda i, j: (i, 0))
    vt = v.astype(BF16).reshape(E // ch, ch, D).transpose(0, 2, 1)
    u_window = pl.BlockSpec((pl.Element(te), pl.Element(D)),
                            lambda i, j: (jnp.minimum(j * (te // ch) + 1, (E - te) // ch) * ch, 0))
    return pl.pallas_call(
        functools.partial(_peer_dense_kernel, final_norm=final_norm),
        grid=(T // tm, E // te),
        in_specs=[pl.BlockSpec((D, tm), lambda i, j: (0, i)),
                  pl.BlockSpec((ch, D), lambda i, j: (0, 0)),
                  u_window,
                  pl.BlockSpec((te // ch, D, ch), lambda i, j: (j, 0, 0)),
                  head_major, head_major, head_major, head_major, row,
                  pl.BlockSpec((1, D), lambda i, j: (0, 0))],
        out_specs=row,
        out_shape=jax.ShapeDtypeStruct((T, D), F32),
        scratch_shapes=[pltpu.VMEM((D, tm), F32), pltpu.VMEM((2, ch, tm), F32)],
        compiler_params=_params("parallel", "arbitrary"),
    )(hnt, u, u, vt, nsel, c1, rank2, e2, x2, g_final)


def _block_diag(w):
    nb, bi, bo = w.shape
    eye = jnp.eye(nb, dtype=w.dtype)
    return (eye[:, None, :, None] * w[:, :, None, :]).reshape(nb * bi, nb * bo)


def kernel(x, mem, norm_mix, w_in, conv_w, conv_b, gate_a_w, gate_a_b, gate_x_w, gate_x_b, lru_lambda, rel_bias, norm_grp_a, norm_grp_b, w_out, norm_cross, norm_mem, w_q_mem, w_kv_mem, w_o_mem, norm_ffn, w_query, sub_keys, expert_u, expert_v, norm_final):
    B, S, D = x.shape
    depth = w_in.shape[0]
    lru_w = conv_w.shape[-1]
    att_w = (w_in.shape[-1] - 2 * lru_w) // 3
    row = lambda v: v.reshape(1, -1)

    cur = x.reshape(B * S, D)
    mem2d = mem.reshape(B * mem.shape[1], D)
    for l in range(depth):
        w_gates = jnp.concatenate([_block_diag(gate_a_w[l]), _block_diag(gate_x_w[l])],
                                  axis=1).astype(BF16)
        b_gates = jnp.concatenate([gate_a_b[l], gate_x_b[l]]).reshape(1, -1)
        w_in16 = w_in[l].astype(BF16)
        ya, q, k, v = _in_lru(cur, B, row(norm_mix[l]), w_in16[:, :2 * lru_w],
                              w_in16[:, 2 * lru_w:], conv_w[l], row(conv_b[l]), w_gates,
                              b_gates, row(lru_lambda[l]), row(norm_grp_a[l]))
        yb = _chunk_attn(q, k, v, B, _attention_bias(rel_bias[l]), row(norm_grp_b[l]))
        kmem, vmem = _mem_kv(mem2d, B, row(norm_mem[l]), w_kv_mem[l].astype(BF16))
        wo = w_out[l].astype(BF16)
        x2, hn3, qp = _mix_cross(cur, ya, yb, wo[:lru_w], wo[lru_w:], row(norm_cross[l]),
                                 w_q_mem[l].astype(BF16), kmem, vmem,
                                 w_o_mem[l].astype(BF16), row(norm_ffn[l]),
                                 w_query[l].astype(BF16), B)
        keys = sub_keys[l].reshape(2 * PEER_HEADS, N_KEYS, -1).astype(BF16)
        nsel, c1, rank2, e2 = _peer_route(qp, keys)
        cur = _peer_dense(hn3, expert_u[l], expert_v[l], nsel, c1, rank2, e2, x2,
                          row(norm_final), l == depth - 1)
    return cur.reshape(B, S, D)
```
